```python
import jax, jax.numpy as jnp
from jax import lax
import numpy as np

D_MODEL = 1024
BATCH = 2
SEQ = 8192
DEPTH = 4

CHUNK = 64
Q_BLOCK = 128
N_BRANCHES = 3
BRANCH_WIDTH = 512
SB_HEADS = 8
SB_HEAD_DIM = BRANCH_WIDTH // SB_HEADS
CONV_CHANNELS = BRANCH_WIDTH
CONV_WIDTH = 31
GLA_HEADS = 4
GLA_KEY_DIM = BRANCH_WIDTH // 2
GLA_VALUE_DIM = BRANCH_WIDTH
GLA_HEAD_K = GLA_KEY_DIM // GLA_HEADS
GLA_HEAD_V = GLA_VALUE_DIM // GLA_HEADS
GLA_GATE_RANK = 16
GLA_GATE_TAU = 16.0
D_FF = 2816
NORM_EPS = 1e-6

IN_WIDTHS = (
    BRANCH_WIDTH, BRANCH_WIDTH, BRANCH_WIDTH,
    2 * CONV_CHANNELS,
    GLA_KEY_DIM, GLA_KEY_DIM, GLA_VALUE_DIM,
    GLA_VALUE_DIM,
    GLA_GATE_RANK,
    N_BRANCHES * D_MODEL,
)
IN_WIDTH = sum(IN_WIDTHS)

kernel_name = "hybrid_stickbreak_conformer_gla_trunk"


def rms_norm(x, g):
    xf = x.astype(jnp.float32)
    y = xf * lax.rsqrt(jnp.mean(xf * xf, axis=-1, keepdims=True) + NORM_EPS)
    return (y * g.astype(jnp.float32)).astype(x.dtype)


def swiglu_ffn(h, w_gate, w_up, w_down):
    return (jax.nn.silu(h @ w_gate) * (h @ w_up)) @ w_down


def stick_breaking_attention(q, k, v):
    b, nh, s, dh = q.shape
    nb = s // Q_BLOCK
    q_blocks = q.reshape(b, nh, nb, Q_BLOCK, dh).transpose(2, 0, 1, 3, 4)
    key_pos = jnp.arange(s)
    scale = dh ** -0.5

    def one_block(args):
        q_blk, blk = args
        z = jnp.einsum('bhqd,bhkd->bhqk', q_blk, k).astype(jnp.float32) * scale
        query_pos = blk * Q_BLOCK + jnp.arange(Q_BLOCK)
        earlier = key_pos[None, :] < query_pos[:, None]
        log_keep = jnp.where(earlier, jax.nn.log_sigmoid(-z), 0.0)
        log_stick = lax.cumsum(log_keep, axis=3, reverse=True) - log_keep
        w = jnp.where(earlier, jnp.exp(jax.nn.log_sigmoid(z) + log_stick), 0.0)
        return jnp.einsum('bhqk,bhkd->bhqd', w.astype(v.dtype), v)

    out = lax.map(one_block, (q_blocks, jnp.arange(nb)))
    return out.transpose(1, 2, 0, 3, 4).reshape(b, nh, s, dh)


def conformer_conv(u_glu, conv_w, conv_b, ln_g, ln_b):
    a, g = jnp.split(u_glu, 2, axis=-1)
    u = a * jax.nn.sigmoid(g)
    y = lax.conv_general_dilated(
        u, conv_w[:, None, :], window_strides=(1,),
        padding=[(CONV_WIDTH - 1, 0)],
        dimension_numbers=('NWC', 'WIO', 'NWC'),
        feature_group_count=CONV_CHANNELS) + conv_b
    yf = y.astype(jnp.float32)
    mu = jnp.mean(yf, axis=-1, keepdims=True)
    var = jnp.mean(jnp.square(yf - mu), axis=-1, keepdims=True)
    yn = (yf - mu) * lax.rsqrt(var + NORM_EPS) * ln_g.astype(jnp.float32) + ln_b.astype(jnp.float32)
    return jax.nn.silu(yn).astype(u.dtype)


def gla_chunked(q, k, v, log_alpha):
    b, s, nh, dk = q.shape
    dv = v.shape[-1]
    n = s // CHUNK

    def to_chunks(t):
        return t.reshape(b, n, CHUNK, nh, t.shape[-1]).transpose(1, 0, 2, 3, 4)

    la_c = to_chunks(log_alpha.astype(jnp.float32))
    decay_to_end = lax.cumsum(la_c, axis=2, reverse=True) - la_c
    chunk_decay = jnp.exp(jnp.sum(la_c, axis=2))
    k_dec = to_chunks(k).astype(jnp.float32) * jnp.exp(decay_to_end)
    q_c = to_chunks(q).astype(jnp.float32) * (dk ** -0.5)
    v_c = to_chunks(v).astype(jnp.float32)

    def step(state, xs):
        q_n, k_n, v_n, lam = xs
        state = lam[..., None] * state + jnp.einsum('bchk,bchv->bhkv', k_n, v_n)
        return state, jnp.einsum('bchk,bhkv->bchv', q_n, state)

    state0 = jnp.zeros((b, nh, dk, dv), jnp.float32)
    _, o = lax.scan(step, state0, (q_c, k_dec, v_c, chunk_decay))
    return o.transpose(1, 0, 2, 3, 4).reshape(b, s, nh, dv).astype(v.dtype)


def hybrid_mixer(h, w_in, conv_w, conv_b, conv_ln_g, conv_ln_b,
                 gla_w_alpha, gla_b_alpha, gla_norm_g, w_branch, w_out):
    b, s, _ = h.shape
    proj = h @ w_in
    offsets = [int(o) for o in np.cumsum(IN_WIDTHS)[:-1]]
    (sb_q, sb_k, sb_v, conv_in, gla_q, gla_k, gla_v, gla_r, gla_lr,
     gate_logits) = jnp.split(proj, offsets, axis=-1)

    def heads(t, nh):
        return t.reshape(b, s, nh, -1)

    sb_out = stick_breaking_attention(
        heads(sb_q, SB_HEADS).transpose(0, 2, 1, 3),
        heads(sb_k, SB_HEADS).transpose(0, 2, 1, 3),
        heads(sb_v, SB_HEADS).transpose(0, 2, 1, 3))
    sb_out = sb_out.transpose(0, 2, 1, 3).reshape(b, s, BRANCH_WIDTH)

    conv_out = conformer_conv(conv_in, conv_w, conv_b, conv_ln_g, conv_ln_b)

    log_alpha = jax.nn.log_sigmoid((gla_lr @ gla_w_alpha + gla_b_alpha).astype(jnp.float32)) / GLA_GATE_TAU
    gla_o = gla_chunked(heads(gla_q, GLA_HEADS), heads(gla_k, GLA_HEADS),
                        heads(gla_v, GLA_HEADS), heads(log_alpha, GLA_HEADS))
    gla_o = rms_norm(gla_o, gla_norm_g.reshape(GLA_HEADS, GLA_HEAD_V))
    gla_out = gla_o.reshape(b, s, GLA_VALUE_DIM) * jax.nn.silu(gla_r)

    branches = jnp.stack([sb_out, conv_out, gla_out], axis=2)
    branch_d = jnp.einsum('bsgw,gwd->bsgd', branches, w_branch)
    gates = jax.nn.sigmoid(gate_logits.reshape(b, s, N_BRANCHES, D_MODEL))
    merged = jnp.sum(gates * branch_d, axis=2)
    return merged @ w_out


def setup_inputs(seed: int = 0) -> dict:
    key = jax.random.key(seed)
    ks = jax.random.split(key, 20)

    def nrm(k, shape, scale):
        return jax.random.normal(k, shape, jnp.float32) * scale

    L, D = DEPTH, D_MODEL
    return {
        "x": nrm(ks[0], (BATCH, SEQ, D), 1.0),
        "norm_pre": 1.0 + nrm(ks[1], (L, 3, D), 0.05),
        "norm_post": 1.0 + nrm(ks[2], (L, 3, D), 0.05),
        "ffn1_w_gate": nrm(ks[3], (L, D, D_FF), D ** -0.5),
        "ffn1_w_up": nrm(ks[4], (L, D, D_FF), D ** -0.5),
        "ffn1_w_down": nrm(ks[5], (L, D_FF, D), D_FF ** -0.5),
        "ffn2_w_gate": nrm(ks[6], (L, D, D_FF), D ** -0.5),
        "ffn2_w_up": nrm(ks[7], (L, D, D_FF), D ** -0.5),
        "ffn2_w_down": nrm(ks[8], (L, D_FF, D), D_FF ** -0.5),
        "w_in": nrm(ks[9], (L, D, IN_WIDTH), D ** -0.5),
        "conv_w": nrm(ks[10], (L, CONV_WIDTH, CONV_CHANNELS), CONV_WIDTH ** -0.5),
        "conv_b": nrm(ks[11], (L, CONV_CHANNELS), 0.02),
        "conv_ln_g": 1.0 + nrm(ks[12], (L, CONV_CHANNELS), 0.05),
        "conv_ln_b": nrm(ks[13], (L, CONV_CHANNELS), 0.02),
        "gla_w_alpha": nrm(ks[14], (L, GLA_GATE_RANK, GLA_KEY_DIM), GLA_GATE_RANK ** -0.5),
        "gla_b_alpha": nrm(ks[15], (L, GLA_KEY_DIM), 0.02),
        "gla_norm_g": 1.0 + nrm(ks[16], (L, GLA_VALUE_DIM), 0.05),
        "w_branch": nrm(ks[17], (L, N_BRANCHES, BRANCH_WIDTH, D), BRANCH_WIDTH ** -0.5),
        "w_out": nrm(ks[18], (L, D, D), D ** -0.5),
    }


def reference(x, norm_pre, norm_post, ffn1_w_gate, ffn1_w_up, ffn1_w_down,
              ffn2_w_gate, ffn2_w_up, ffn2_w_down, w_in, conv_w, conv_b,
              conv_ln_g, conv_ln_b, gla_w_alpha, gla_b_alpha, gla_norm_g,
              w_branch, w_out):
    for l in range(DEPTH):
        h = rms_norm(x, norm_pre[l, 0])
        x = x + 0.5 * rms_norm(swiglu_ffn(h, ffn1_w_gate[l], ffn1_w_up[l], ffn1_w_down[l]), norm_post[l, 0])
        h = rms_norm(x, norm_pre[l, 1])
        m = hybrid_mixer(h, w_in[l], conv_w[l], conv_b[l], conv_ln_g[l], conv_ln_b[l],
                         gla_w_alpha[l], gla_b_alpha[l], gla_norm_g[l], w_branch[l], w_out[l])
        x = x + rms_norm(m, norm_post[l, 1])
        h = rms_norm(x, norm_pre[l, 2])
        x = x + 0.5 * rms_norm(swiglu_ffn(h, ffn2_w_gate[l], ffn2_w_up[l], ffn2_w_down[l]), norm_post[l, 2])
    return x
```

```python
import functools

import jax
import jax.numpy as jnp
from jax import lax
from jax.experimental import pallas as pl
from jax.experimental.pallas import tpu as pltpu

NORM_EPS = 1e-6
BRANCH_WIDTH = 512
SB_HEAD_DIM = 64
CONV_WIDTH = 31
GLA_KEY_DIM = 256
GLA_HEADS = 4
GLA_HEAD_K = 64
GLA_HEAD_V = 128
GLA_GATE_RANK = 16
GLA_GATE_TAU = 16.0
GLA_CHUNK = 64

LANES = 128
VMEM_LIMIT_BYTES = 56 * 1024 * 1024

F32 = jnp.float32
BF16 = jnp.bfloat16


def _dot(a, b):
    return jnp.dot(a, b, preferred_element_type=F32)


def _dot_nt(a, b):
    return lax.dot_general(a, b, (((1,), (1,)), ((), ())), preferred_element_type=F32)


def _dot_tn(a, b, precision=None):
    return lax.dot_general(a, b, (((0,), (0,)), ((), ())), preferred_element_type=F32,
                           precision=precision)


def _rms(x, g):
    return x * lax.rsqrt(jnp.mean(x * x, axis=-1, keepdims=True) + NORM_EPS) * g


def _sigmoid(x):
    return 1.0 / (1.0 + jnp.exp(-x))


def _softplus(z):
    return jnp.maximum(z, 0.0) + jnp.log(1.0 + jnp.exp(-jnp.abs(z)))


def _params(*sem):
    return pltpu.CompilerParams(dimension_semantics=sem, vmem_limit_bytes=VMEM_LIMIT_BYTES)


def _resident(shape):
    nd = len(shape)
    return pl.BlockSpec(shape, lambda *_: (0,) * nd, pipeline_mode=pl.Buffered(1))


def _ffn_body(x_ref, pre_ref, post_ref, wg_ref, wu_ref, wd_ref, o_ref, acc_ref, *, fc):
    x = x_ref[...]
    h = _rms(x, pre_ref[...]).astype(BF16)
    d_ff = wg_ref.shape[1]
    for c0 in range(0, d_ff, fc):
        c1 = min(c0 + fc, d_ff)
        g = _dot(h, wg_ref[:, c0:c1])
        u = _dot(h, wu_ref[:, c0:c1])
        a = (g * _sigmoid(g) * u).astype(BF16)
        d = _dot(a, wd_ref[c0:c1, :])
        if c0 == 0:
            acc_ref[...] = d
        else:
            acc_ref[...] += d
    o_ref[...] = x + 0.5 * _rms(acc_ref[...], post_ref[...])


def _ffn(x, pre, post, wg, wu, wd, *, tm=512, fc=512):
    t, d = x.shape
    d_ff = wg.shape[1]
    row = pl.BlockSpec((tm, d), lambda i: (i, 0))
    return pl.pallas_call(
        functools.partial(_ffn_body, fc=fc),
        grid=(t // tm,),
        in_specs=[row, _resident((1, d)), _resident((1, d)),
                  _resident((d, d_ff)), _resident((d, d_ff)), _resident((d_ff, d))],
        out_specs=row,
        out_shape=jax.ShapeDtypeStruct((t, d), F32),
        scratch_shapes=[pltpu.VMEM((tm, d), F32)],
        compiler_params=_params("parallel"),
        name="ffn",
    )(x, pre.reshape(1, d), post.reshape(1, d), wg, wu, wd)


def _inproj_body(x_ref, pre_ref, wqkv_ref, wconv_ref, wgla_ref, wlr_ref, wgate_ref,
                 walpha_ref, balpha_ref,
                 q_ref, k_ref, v_ref, u_ref, gq_ref, gk_ref, gv_ref, gr_ref, la_ref, gates_ref):
    w = BRANCH_WIDTH
    dk = GLA_KEY_DIM
    h = _rms(x_ref[...], pre_ref[...]).astype(BF16)

    qkv = _dot(h, wqkv_ref[...])
    q_ref[...] = (qkv[:, :w] * (SB_HEAD_DIM ** -0.5)).astype(BF16)
    k_ref[...] = qkv[:, w:2 * w].astype(BF16)
    v_ref[...] = qkv[:, 2 * w:].astype(BF16)

    c = _dot(h, wconv_ref[...])
    u_ref[...] = (c[:, :w] * _sigmoid(c[:, w:])).astype(BF16)

    g4 = _dot(h, wgla_ref[...])
    gq_ref[...] = (g4[:, :dk] * (GLA_HEAD_K ** -0.5)).astype(BF16)
    gk_ref[...] = g4[:, dk:2 * dk].astype(BF16)
    gv_ref[...] = g4[:, 2 * dk:2 * dk + w].astype(BF16)
    gr_ref[...] = g4[:, 2 * dk + w:].astype(BF16)

    lr = _dot(h, wlr_ref[...])
    a = jnp.dot(lr, walpha_ref[...], preferred_element_type=F32,
                precision=lax.Precision.HIGHEST) + balpha_ref[...]
    la_ref[...] = (jnp.minimum(a, 0.0) - jnp.log(1.0 + jnp.exp(-jnp.abs(a)))) * (1.0 / GLA_GATE_TAU)

    gates_ref[...] = _sigmoid(_dot(h, wgate_ref[...])).astype(BF16)


def _inproj(x, pre, wqkv, wconv, wgla, wlr, wgate, walpha, balpha, *, tm=512):
    t, d = x.shape
    w = BRANCH_WIDTH
    dk = GLA_KEY_DIM
    n_gate = wgate.shape[1]

    def row(n):
        return pl.BlockSpec((tm, n), lambda i: (i, 0))

    def out(n, dt=BF16):
        return jax.ShapeDtypeStruct((t, n), dt)

    return pl.pallas_call(
        _inproj_body,
        grid=(t // tm,),
        in_specs=[row(d), _resident((1, d)), _resident(wqkv.shape), _resident(wconv.shape),
                  _resident(wgla.shape), _resident(wlr.shape), _resident(wgate.shape),
                  _resident(walpha.shape), _resident((1, dk))],
        out_specs=[row(w), row(w), row(w), row(w), row(dk), row(dk), row(w), row(w), row(dk),
                   row(n_gate)],
        out_shape=[out(w), out(w), out(w), out(w), out(dk), out(dk), out(w), out(w),
                   out(dk, F32), out(n_gate)],
        compiler_params=_params("parallel"),
        name="inproj",
    )(x, pre.reshape(1, d), wqkv, wconv, wgla, wlr, wgate, walpha, balpha.reshape(1, dk))


def _attn_body(q_ref, k_ref, v_ref, o_ref, *, tb):
    i = pl.program_id(2)
    q = q_ref[0]
    lane = lax.broadcasted_iota(jnp.int32, (1, LANES), 1)
    head0 = lane < SB_HEAD_DIM
    qh = (jnp.where(head0, q, jnp.zeros_like(q)), jnp.where(head0, jnp.zeros_like(q), q))

    r = lax.broadcasted_iota(jnp.int32, (tb, tb), 0)
    c = lax.broadcasted_iota(jnp.int32, (tb, tb), 1)
    earlier = c < r
    suffix = jnp.concatenate([jnp.where(r > c, 1.0, 0.0), jnp.ones((tb, tb), F32)], axis=1).astype(BF16)

    def step(kb, vb, carry, masked):
        new = []
        for h in range(2):
            neg_log_stick, acc = carry[h]
            z = _dot_nt(qh[h], kb)
            sp = _softplus(z)
            if masked:
                sp = jnp.where(earlier, sp, 0.0)
            sums = _dot(sp.astype(BF16), suffix)
            wgt = jnp.exp(z - sp - sums[:, :tb] - neg_log_stick)
            if masked:
                wgt = jnp.where(earlier, wgt, 0.0)
            acc = acc + _dot(wgt.astype(BF16), vb)
            new.append((neg_log_stick + sums[:, tb:], acc))
        return tuple(new)

    zero = (jnp.zeros((tb, tb), F32), jnp.zeros((tb, LANES), F32))
    d0 = pl.multiple_of(i * tb, tb)
    carry = step(k_ref[0, pl.ds(d0, tb), :], v_ref[0, pl.ds(d0, tb), :], (zero, zero), True)

    def body(n, carry):
        j0 = pl.multiple_of((i - 1 - n) * tb, tb)
        return step(k_ref[0, pl.ds(j0, tb), :], v_ref[0, pl.ds(j0, tb), :], carry, False)

    carry = lax.fori_loop(0, i, body, carry)
    o_ref[0] = jnp.where(head0, carry[0][1], carry[1][1]).astype(BF16)


def _attn(q, k, v, *, tb=128):
    b, s, w = q.shape
    qspec = pl.BlockSpec((1, tb, LANES), lambda bi, hp, i: (bi, i, hp))
    kvspec = pl.BlockSpec((1, s, LANES), lambda bi, hp, i: (bi, 0, hp))
    return pl.pallas_call(
        functools.partial(_attn_body, tb=tb),
        grid=(b, w // LANES, s // tb),
        in_specs=[qspec, kvspec, kvspec],
        out_specs=qspec,
        out_shape=jax.ShapeDtypeStruct((b, s, w), BF16),
        compiler_params=_params("parallel", "parallel", "parallel"),
        name="stickbreak_attn",
    )(q, k, v)


def _conv_body(u_ref, halo_ref, w_ref, b_ref, g_ref, beta_ref, o_ref, ext_ref, *, tm, rb, halo):
    i = pl.program_id(1)
    ext_ref[halo:halo + tm, :] = u_ref[0].astype(F32)

    @pl.when(i == 0)
    def _():
        ext_ref[0:halo, :] = jnp.zeros((halo, ext_ref.shape[1]), F32)

    @pl.when(i > 0)
    def _():
        ext_ref[0:halo, :] = halo_ref[0].astype(F32)

    w = w_ref[...]
    bias = b_ref[...]
    gain = g_ref[...]
    beta = beta_ref[...]
    first = halo - (CONV_WIDTH - 1)
    for r0 in range(0, tm, rb):
        acc = bias + w[0:1, :] * ext_ref[r0 + first:r0 + first + rb, :]
        for j in range(1, CONV_WIDTH):
            acc = acc + w[j:j + 1, :] * ext_ref[r0 + first + j:r0 + first + j + rb, :]
        mu = jnp.mean(acc, axis=-1, keepdims=True)
        yc = acc - mu
        var = jnp.mean(yc * yc, axis=-1, keepdims=True)
        yn = yc * lax.rsqrt(var + NORM_EPS) * gain + beta
        o_ref[0, r0:r0 + rb, :] = (yn * _sigmoid(yn)).astype(BF16)


def _conv(u, w, bias, gain, beta, *, tm=256, rb=32, halo=32):
    b, s, ch = u.shape
    per = tm // halo
    vec = _resident((1, ch))
    return pl.pallas_call(
        functools.partial(_conv_body, tm=tm, rb=rb, halo=halo),
        grid=(b, s // tm),
        in_specs=[pl.BlockSpec((1, tm, ch), lambda bi, i: (bi, i, 0)),
                  pl.BlockSpec((1, halo, ch), lambda bi, i: (bi, jnp.maximum(i * per - 1, 0), 0)),
                  _resident((CONV_WIDTH, ch)), vec, vec, vec],
        out_specs=pl.BlockSpec((1, tm, ch), lambda bi, i: (bi, i, 0)),
        out_shape=jax.ShapeDtypeStruct((b, s, ch), BF16),
        scratch_shapes=[pltpu.VMEM((tm + halo, ch), F32)],
        compiler_params=_params("parallel", "parallel"),
        name="conv_module",
    )(u, u, w, bias.reshape(1, ch), gain.reshape(1, ch), beta.reshape(1, ch))


def _gla_body(q_ref, k_ref, v_ref, r_ref, la_ref, g_ref, o_ref, state_ref, *, tm):
    ch = GLA_CHUNK
    dk = GLA_KEY_DIM
    dv = v_ref.shape[2]

    @pl.when(pl.program_id(1) == 0)
    def _():
        state_ref[...] = jnp.zeros_like(state_ref)

    rr = lax.broadcasted_iota(jnp.int32, (ch, ch), 0)
    cc = lax.broadcasted_iota(jnp.int32, (ch, ch), 1)
    later = jnp.where(cc > rr, 1.0, 0.0)
    ones = jnp.ones((ch, LANES), F32)
    sr = lax.broadcasted_iota(jnp.int32, (dk, dv), 0) // GLA_HEAD_K
    sc = lax.broadcasted_iota(jnp.int32, (dk, dv), 1) // GLA_HEAD_V
    same_head = sr == sc
    gain = g_ref[...]
    hi = lax.Precision.HIGHEST

    def chunk(n, carry):
        r0 = pl.multiple_of(n * ch, ch)
        rows = pl.ds(r0, ch)
        la = la_ref[0, rows, :]
        decay_to_end = jnp.dot(later, la, preferred_element_type=F32, precision=hi)
        chunk_log_decay = _dot_tn(la, ones, precision=hi)
        lam = jnp.exp(chunk_log_decay)
        k_dec = (k_ref[0, rows, :].astype(F32) * jnp.exp(decay_to_end)).astype(BF16)
        kv = _dot_tn(k_dec, v_ref[0, rows, :])
        st = state_ref[...] * jnp.concatenate([lam] * (dv // LANES), axis=1) + jnp.where(same_head, kv, 0.0)
        state_ref[...] = st
        o = _dot(q_ref[0, rows, :], st.astype(BF16))
        normed = []
        for h in range(GLA_HEADS):
            oh = o[:, h * GLA_HEAD_V:(h + 1) * GLA_HEAD_V]
            normed.append(oh * lax.rsqrt(jnp.mean(oh * oh, axis=-1, keepdims=True) + NORM_EPS))
        rg = r_ref[0, rows, :].astype(F32)
        o_ref[0, rows, :] = (jnp.concatenate(normed, axis=1) * gain * (rg * _sigmoid(rg))).astype(BF16)
        return carry

    lax.fori_loop(0, tm // ch, chunk, 0)


def _gla(q, k, v, r, la, gain, *, tm=512):
    b, s, dk = q.shape
    dv = v.shape[2]

    def blk(n):
        return pl.BlockSpec((1, tm, n), lambda bi, i: (bi, i, 0))

    return pl.pallas_call(
        functools.partial(_gla_body, tm=tm),
        grid=(b, s // tm),
        in_specs=[blk(dk), blk(dk), blk(dv), blk(dv), blk(dk), _resident((1, dv))],
        out_specs=blk(dv),
        out_shape=jax.ShapeDtypeStruct((b, s, dv), BF16),
        scratch_shapes=[pltpu.VMEM((dk, dv), F32)],
        compiler_params=_params("parallel", "arbitrary"),
        name="gla",
    )(q, k, v, r, la, gain.reshape(1, dv))


def _merge_body(x_ref, sb_ref, cv_ref, gl_ref, gates_ref, wb_ref, wo_ref, post_ref, o_ref):
    d = x_ref.shape[1]
    merged = gates_ref[:, 0:d].astype(F32) * _dot(sb_ref[...], wb_ref[0])
    merged = merged + gates_ref[:, d:2 * d].astype(F32) * _dot(cv_ref[...], wb_ref[1])
    merged = merged + gates_ref[:, 2 * d:3 * d].astype(F32) * _dot(gl_ref[...], wb_ref[2])
    m = _dot(merged.astype(BF16), wo_ref[...])
    o_ref[...] = x_ref[...] + _rms(m, post_ref[...])


def _merge(x, sb, cv, gl, gates, wb, wo, post, *, tm=512):
    t, d = x.shape
    w = sb.shape[1]

    def row(n):
        return pl.BlockSpec((tm, n), lambda i: (i, 0))

    return pl.pallas_call(
        _merge_body,
        grid=(t // tm,),
        in_specs=[row(d), row(w), row(w), row(w), row(3 * d), _resident(wb.shape),
                  _resident(wo.shape), _resident((1, d))],
        out_specs=row(d),
        out_shape=jax.ShapeDtypeStruct((t, d), F32),
        compiler_params=_params("parallel"),
        name="merge",
    )(x, sb, cv, gl, gates, wb, wo, post.reshape(1, d))


def _mixer(x, b, s, pre, post, w_in, conv_w, conv_b, conv_ln_g, conv_ln_b,
           gla_w_alpha, gla_b_alpha, gla_norm_g, w_branch, w_out):
    t, d = x.shape
    w = BRANCH_WIDTH
    dk = GLA_KEY_DIM
    o_conv = 3 * w
    o_gla = o_conv + 2 * w
    o_lr = o_gla + 2 * dk + 2 * w
    o_gate = o_lr + GLA_GATE_RANK
    wqkv = w_in[:, :o_conv].astype(BF16)
    wconv = w_in[:, o_conv:o_gla].astype(BF16)
    wgla = w_in[:, o_gla:o_lr].astype(BF16)
    wlr = jnp.pad(w_in[:, o_lr:o_gate], ((0, 0), (0, LANES - GLA_GATE_RANK))).astype(BF16)
    wgate = w_in[:, o_gate:].astype(BF16)
    walpha = jnp.pad(gla_w_alpha, ((0, LANES - GLA_GATE_RANK), (0, 0)))

    q, k, v, u, gq, gk, gv, gr, la, gates = _inproj(
        x, pre, wqkv, wconv, wgla, wlr, wgate, walpha, gla_b_alpha)

    def seq(a):
        return a.reshape(b, s, a.shape[1])

    sb = _attn(seq(q), seq(k), seq(v)).reshape(t, w)
    cv = _conv(seq(u), conv_w, conv_b, conv_ln_g, conv_ln_b).reshape(t, w)
    gl = _gla(seq(gq), seq(gk), seq(gv), seq(gr), seq(la), gla_norm_g).reshape(t, w)
    return _merge(x, sb, cv, gl, gates, w_branch.astype(BF16), w_out.astype(BF16), post)


def kernel(x, norm_pre, norm_post, ffn1_w_gate, ffn1_w_up, ffn1_w_down, ffn2_w_gate, ffn2_w_up,
           ffn2_w_down, w_in, conv_w, conv_b, conv_ln_g, conv_ln_b, gla_w_alpha, gla_b_alpha,
           gla_norm_g, w_branch, w_out):
    b, s, d = x.shape
    xf = x.reshape(b * s, d)
    for l in range(norm_pre.shape[0]):
        xf = _ffn(xf, norm_pre[l, 0], norm_post[l, 0], ffn1_w_gate[l].astype(BF16),
                  ffn1_w_up[l].astype(BF16), ffn1_w_down[l].astype(BF16))
        xf = _mixer(xf, b, s, norm_pre[l, 1], norm_post[l, 1], w_in[l], conv_w[l], conv_b[l],
                    conv_ln_g[l], conv_ln_b[l], gla_w_alpha[l], gla_b_alpha[l], gla_norm_g[l],
                    w_branch[l], w_out[l])
        xf = _ffn(xf, norm_pre[l, 2], norm_post[l, 2], ffn2_w_gate[l].astype(BF16),
                  ffn2_w_up[l].astype(BF16), ffn2_w_down[l].astype(BF16))
    return xf.reshape(b, s, d)
```

```python
import functools

import jax
import jax.numpy as jnp
from jax import lax
from jax.experimental import pallas as pl
from jax.experimental.pallas import tpu as pltpu

NORM_EPS = 1e-6
BRANCH_WIDTH = 512
SB_HEAD_DIM = 64
CONV_WIDTH = 31
GLA_KEY_DIM = 256
GLA_HEADS = 4
GLA_HEAD_K = 64
GLA_HEAD_V = 128
GLA_GATE_RANK = 16
GLA_GATE_TAU = 16.0
GLA_CHUNK = 64

DEAD_STICK = 110.0

LANES = 128
VMEM_LIMIT_BYTES = 56 * 1024 * 1024

F32 = jnp.float32
BF16 = jnp.bfloat16


def _dot(a, b):
    return jnp.dot(a, b, preferred_element_type=F32)


def _dot_nt(a, b):
    return lax.dot_general(a, b, (((1,), (1,)), ((), ())), preferred_element_type=F32)


def _dot_tn(a, b, precision=None):
    return lax.dot_general(a, b, (((0,), (0,)), ((), ())), preferred_element_type=F32,
                           precision=precision)


def _rms(x, g):
    return x * lax.rsqrt(jnp.mean(x * x, axis=-1, keepdims=True) + NORM_EPS) * g


def _sigmoid(x):
    return 1.0 / (1.0 + jnp.exp(-x))


def _softplus(z):
    return jnp.maximum(z, 0.0) + jnp.log(1.0 + jnp.exp(-jnp.abs(z)))


def _params(*sem):
    return pltpu.CompilerParams(dimension_semantics=sem, vmem_limit_bytes=VMEM_LIMIT_BYTES)


def _resident(shape):
    nd = len(shape)
    return pl.BlockSpec(shape, lambda *_: (0,) * nd, pipeline_mode=pl.Buffered(1))


def _ffn_body(x_ref, pre_ref, post_ref, wg_ref, wu_ref, wd_ref, o_ref, acc_ref, *, fc):
    x = x_ref[...]
    h = _rms(x, pre_ref[...]).astype(BF16)
    d_ff = wg_ref.shape[1]
    for c0 in range(0, d_ff, fc):
        c1 = min(c0 + fc, d_ff)
        g = _dot(h, wg_ref[:, c0:c1])
        u = _dot(h, wu_ref[:, c0:c1])
        a = (g * _sigmoid(g) * u).astype(BF16)
        d = _dot(a, wd_ref[c0:c1, :])
        if c0 == 0:
            acc_ref[...] = d
        else:
            acc_ref[...] += d
    o_ref[...] = x + 0.5 * _rms(acc_ref[...], post_ref[...])


def _ffn(x, pre, post, wg, wu, wd, *, tm=512, fc=512):
    t, d = x.shape
    d_ff = wg.shape[1]
    row = pl.BlockSpec((tm, d), lambda i: (i, 0))
    return pl.pallas_call(
        functools.partial(_ffn_body, fc=fc),
        grid=(t // tm,),
        in_specs=[row, _resident((1, d)), _resident((1, d)),
                  _resident((d, d_ff)), _resident((d, d_ff)), _resident((d_ff, d))],
        out_specs=row,
        out_shape=jax.ShapeDtypeStruct((t, d), F32),
        scratch_shapes=[pltpu.VMEM((tm, d), F32)],
        compiler_params=_params("parallel"),
        name="ffn",
    )(x, pre.reshape(1, d), post.reshape(1, d), wg, wu, wd)


def _inproj_body(x_ref, pre_ref, wqkv_ref, wconv_ref, wgla_ref, wlr_ref, wgate_ref,
                 walpha_ref, balpha_ref,
                 q_ref, k_ref, v_ref, u_ref, gq_ref, gk_ref, gv_ref, gr_ref, la_ref, gates_ref):
    w = BRANCH_WIDTH
    dk = GLA_KEY_DIM
    h = _rms(x_ref[...], pre_ref[...]).astype(BF16)

    qkv = _dot(h, wqkv_ref[...])
    q_ref[...] = (qkv[:, :w] * (SB_HEAD_DIM ** -0.5)).astype(BF16)
    k_ref[...] = qkv[:, w:2 * w].astype(BF16)
    v_ref[...] = qkv[:, 2 * w:].astype(BF16)

    c = _dot(h, wconv_ref[...])
    u_ref[...] = (c[:, :w] * _sigmoid(c[:, w:])).astype(BF16)

    g4 = _dot(h, wgla_ref[...])
    gq_ref[...] = (g4[:, :dk] * (GLA_HEAD_K ** -0.5)).astype(BF16)
    gk_ref[...] = g4[:, dk:2 * dk].astype(BF16)
    gv_ref[...] = g4[:, 2 * dk:2 * dk + w].astype(BF16)
    gr_ref[...] = g4[:, 2 * dk + w:].astype(BF16)

    lr = _dot(h, wlr_ref[...])
    a = jnp.dot(lr, walpha_ref[...], preferred_element_type=F32,
                precision=lax.Precision.HIGHEST) + balpha_ref[...]
    la_ref[...] = (jnp.minimum(a, 0.0) - jnp.log(1.0 + jnp.exp(-jnp.abs(a)))) * (1.0 / GLA_GATE_TAU)

    gates_ref[...] = _sigmoid(_dot(h, wgate_ref[...])).astype(BF16)


def _inproj(x, pre, wqkv, wconv, wgla, wlr, wgate, walpha, balpha, *, tm=512):
    t, d = x.shape
    w = BRANCH_WIDTH
    dk = GLA_KEY_DIM
    n_gate = wgate.shape[1]

    def row(n):
        return pl.BlockSpec((tm, n), lambda i: (i, 0))

    def out(n, dt=BF16):
        return jax.ShapeDtypeStruct((t, n), dt)

    return pl.pallas_call(
        _inproj_body,
        grid=(t // tm,),
        in_specs=[row(d), _resident((1, d)), _resident(wqkv.shape), _resident(wconv.shape),
                  _resident(wgla.shape), _resident(wlr.shape), _resident(wgate.shape),
                  _resident(walpha.shape), _resident((1, dk))],
        out_specs=[row(w), row(w), row(w), row(w), row(dk), row(dk), row(w), row(w), row(dk),
                   row(n_gate)],
        out_shape=[out(w), out(w), out(w), out(w), out(dk), out(dk), out(w), out(w),
                   out(dk, F32), out(n_gate)],
        compiler_params=_params("parallel"),
        name="inproj",
    )(x, pre.reshape(1, d), wqkv, wconv, wgla, wlr, wgate, walpha, balpha.reshape(1, dk))


def _attn_body(q_ref, k_ref, v_ref, o_ref, stick_ref, acc_ref, *, tb, window):
    i = pl.program_id(1)
    n_pairs = q_ref.shape[2] // LANES
    lane = lax.broadcasted_iota(jnp.int32, (1, LANES), 1)
    head0 = lane < SB_HEAD_DIM
    qh = []
    for p in range(n_pairs):
        q = q_ref[0, :, p * LANES:(p + 1) * LANES]
        qh.append(jnp.where(head0, q, jnp.zeros_like(q)))
        qh.append(jnp.where(head0, jnp.zeros_like(q), q))
    n_heads = len(qh)

    r = lax.broadcasted_iota(jnp.int32, (tb, tb), 0)
    c = lax.broadcasted_iota(jnp.int32, (tb, tb), 1)
    earlier = c < r
    suffix = jnp.concatenate([jnp.where(r > c, 1.0, 0.0), jnp.ones((tb, tb), F32)], axis=1).astype(BF16)

    def step(j0, carry, masked):
        new = []
        for p in range(n_pairs):
            kb = k_ref[0, pl.ds(j0, tb), p * LANES:(p + 1) * LANES]
            vb = v_ref[0, pl.ds(j0, tb), p * LANES:(p + 1) * LANES]
            for h in (2 * p, 2 * p + 1):
                neg_log_stick, acc = carry[h]
                z = _dot_nt(qh[h], kb)
                sp = _softplus(z)
                if masked:
                    sp = jnp.where(earlier, sp, 0.0)
                sums = _dot(sp.astype(BF16), suffix)
                wgt = jnp.exp(z - sp - sums[:, :tb] - neg_log_stick)
                if masked:
                    wgt = jnp.where(earlier, wgt, 0.0)
                new.append((neg_log_stick + sums[:, tb:], acc + _dot(wgt.astype(BF16), vb)))
        return new

    def any_alive(carry):
        m = carry[0][0]
        for h in range(1, n_heads):
            m = jnp.minimum(m, carry[h][0])
        return jnp.min(m) < DEAD_STICK

    def write_out(accs):
        for p in range(n_pairs):
            o_ref[0, :, p * LANES:(p + 1) * LANES] = jnp.where(
                head0, accs[2 * p], accs[2 * p + 1]).astype(BF16)

    def finish(carry, next_block):
        write_out([acc for _, acc in carry])

        @pl.when(jnp.logical_and(next_block >= 0, any_alive(carry)))
        def _():
            for h in range(n_heads):
                stick_ref[h] = carry[h][0]
                acc_ref[h] = carry[h][1]

            def cond(st):
                return jnp.logical_and(st[0] >= 0, st[1] > 0)

            def body(st):
                j = st[0]
                cur = [(stick_ref[h], acc_ref[h]) for h in range(n_heads)]
                cur = step(pl.multiple_of(j * tb, tb), cur, False)
                for h in range(n_heads):
                    stick_ref[h] = cur[h][0]
                    acc_ref[h] = cur[h][1]
                return j - 1, any_alive(cur).astype(jnp.int32)

            lax.while_loop(cond, body, (next_block, jnp.int32(1)))
            write_out([acc_ref[h] for h in range(n_heads)])

    zero = [(jnp.zeros((tb, tb), F32), jnp.zeros((tb, LANES), F32))] * n_heads
    d0 = pl.multiple_of(i * tb, tb)

    @pl.when(i >= window - 1)
    def _():
        carry = step(d0, zero, True)
        for n in range(1, window):
            carry = step(pl.multiple_of(d0 - n * tb, tb), carry, False)
        finish(carry, i - window)

    @pl.when(i < window - 1)
    def _():
        finish(step(d0, zero, True), i - 1)


def _attn(q, k, v, *, tb=128, window=3):
    b, s, w = q.shape
    n_heads = w // SB_HEAD_DIM
    qspec = pl.BlockSpec((1, tb, w), lambda bi, i: (bi, i, 0))
    kvspec = pl.BlockSpec((1, s, w), lambda bi, i: (bi, 0, 0))
    return pl.pallas_call(
        functools.partial(_attn_body, tb=tb, window=window),
        grid=(b, s // tb),
        in_specs=[qspec, kvspec, kvspec],
        out_specs=qspec,
        out_shape=jax.ShapeDtypeStruct((b, s, w), BF16),
        scratch_shapes=[pltpu.VMEM((n_heads, tb, tb), F32), pltpu.VMEM((n_heads, tb, LANES), F32)],
        compiler_params=_params("parallel", "parallel"),
        name="stickbreak_attn",
    )(q, k, v)


def _conv_body(u_ref, halo_ref, w_ref, b_ref, g_ref, beta_ref, o_ref, ext_ref, *, tm, rb, halo):
    i = pl.program_id(1)
    ext_ref[halo:halo + tm, :] = u_ref[0].astype(F32)

    @pl.when(i == 0)
    def _():
        ext_ref[0:halo, :] = jnp.zeros((halo, ext_ref.shape[1]), F32)

    @pl.when(i > 0)
    def _():
        ext_ref[0:halo, :] = halo_ref[0].astype(F32)

    w = w_ref[...]
    bias = b_ref[...]
    gain = g_ref[...]
    beta = beta_ref[...]
    first = halo - (CONV_WIDTH - 1)
    for r0 in range(0, tm, rb):
        acc = bias + w[0:1, :] * ext_ref[r0 + first:r0 + first + rb, :]
        for j in range(1, CONV_WIDTH):
            acc = acc + w[j:j + 1, :] * ext_ref[r0 + first + j:r0 + first + j + rb, :]
        mu = jnp.mean(acc, axis=-1, keepdims=True)
        yc = acc - mu
        var = jnp.mean(yc * yc, axis=-1, keepdims=True)
        yn = yc * lax.rsqrt(var + NORM_EPS) * gain + beta
        o_ref[0, r0:r0 + rb, :] = (yn * _sigmoid(yn)).astype(BF16)


def _conv(u, w, bias, gain, beta, *, tm=256, rb=32, halo=32):
    b, s, ch = u.shape
    per = tm // halo
    vec = _resident((1, ch))
    return pl.pallas_call(
        functools.partial(_conv_body, tm=tm, rb=rb, halo=halo),
        grid=(b, s // tm),
        in_specs=[pl.BlockSpec((1, tm, ch), lambda bi, i: (bi, i, 0)),
                  pl.BlockSpec((1, halo, ch), lambda bi, i: (bi, jnp.maximum(i * per - 1, 0), 0)),
                  _resident((CONV_WIDTH, ch)), vec, vec, vec],
        out_specs=pl.BlockSpec((1, tm, ch), lambda bi, i: (bi, i, 0)),
        out_shape=jax.ShapeDtypeStruct((b, s, ch), BF16),
        scratch_shapes=[pltpu.VMEM((tm + halo, ch), F32)],
        compiler_params=_params("parallel", "parallel"),
        name="conv_module",
    )(u, u, w, bias.reshape(1, ch), gain.reshape(1, ch), beta.reshape(1, ch))


def _gla_body(q_ref, k_ref, v_ref, r_ref, la_ref, g_ref, o_ref, state_ref, *, tm):
    ch = GLA_CHUNK
    dk = GLA_KEY_DIM
    dv = v_ref.shape[2]

    @pl.when(pl.program_id(1) == 0)
    def _():
        state_ref[...] = jnp.zeros_like(state_ref)

    rr = lax.broadcasted_iota(jnp.int32, (ch, ch), 0)
    cc = lax.broadcasted_iota(jnp.int32, (ch, ch), 1)
    later = jnp.where(cc > rr, 1.0, 0.0)
    ones = jnp.ones((ch, LANES), F32)
    sr = lax.broadcasted_iota(jnp.int32, (dk, dv), 0) // GLA_HEAD_K
    sc = lax.broadcasted_iota(jnp.int32, (dk, dv), 1) // GLA_HEAD_V
    same_head = sr == sc
    gain = g_ref[...]
    hi = lax.Precision.HIGHEST

    def chunk(n, carry):
        r0 = pl.multiple_of(n * ch, ch)
        rows = pl.ds(r0, ch)
        la = la_ref[0, rows, :]
        decay_to_end = jnp.dot(later, la, preferred_element_type=F32, precision=hi)
        chunk_log_decay = _dot_tn(la, ones, precision=hi)
        lam = jnp.exp(chunk_log_decay)
        k_dec = (k_ref[0, rows, :].astype(F32) * jnp.exp(decay_to_end)).astype(BF16)
        kv = _dot_tn(k_dec, v_ref[0, rows, :])
        st = state_ref[...] * jnp.concatenate([lam] * (dv // LANES), axis=1) + jnp.where(same_head, kv, 0.0)
        state_ref[...] = st
        o = _dot(q_ref[0, rows, :], st.astype(BF16))
        normed = []
        for h in range(GLA_HEADS):
            oh = o[:, h * GLA_HEAD_V:(h + 1) * GLA_HEAD_V]
            normed.append(oh * lax.rsqrt(jnp.mean(oh * oh, axis=-1, keepdims=True) + NORM_EPS))
        rg = r_ref[0, rows, :].astype(F32)
        o_ref[0, rows, :] = (jnp.concatenate(normed, axis=1) * gain * (rg * _sigmoid(rg))).astype(BF16)
        return carry

    lax.fori_loop(0, tm // ch, chunk, 0)


def _gla(q, k, v, r, la, gain, *, tm=512):
    b, s, dk = q.shape
    dv = v.shape[2]

    def blk(n):
        return pl.BlockSpec((1, tm, n), lambda bi, i: (bi, i, 0))

    return pl.pallas_call(
        functools.partial(_gla_body, tm=tm),
        grid=(b, s // tm),
        in_specs=[blk(dk), blk(dk), blk(dv), blk(dv), blk(dk), _resident((1, dv))],
        out_specs=blk(dv),
        out_shape=jax.ShapeDtypeStruct((b, s, dv), BF16),
        scratch_shapes=[pltpu.VMEM((dk, dv), F32)],
        compiler_params=_params("parallel", "arbitrary"),
        name="gla",
    )(q, k, v, r, la, gain.reshape(1, dv))


def _merge_body(x_ref, sb_ref, cv_ref, gl_ref, gates_ref, wb_ref, wo_ref, post_ref, o_ref):
    d = x_ref.shape[1]
    merged = gates_ref[:, 0:d].astype(F32) * _dot(sb_ref[...], wb_ref[0])
    merged = merged + gates_ref[:, d:2 * d].astype(F32) * _dot(cv_ref[...], wb_ref[1])
    merged = merged + gates_ref[:, 2 * d:3 * d].astype(F32) * _dot(gl_ref[...], wb_ref[2])
    m = _dot(merged.astype(BF16), wo_ref[...])
    o_ref[...] = x_ref[...] + _rms(m, post_ref[...])


def _merge(x, sb, cv, gl, gates, wb, wo, post, *, tm=512):
    t, d = x.shape
    w = sb.shape[1]

    def row(n):
        return pl.BlockSpec((tm, n), lambda i: (i, 0))

    return pl.pallas_call(
        _merge_body,
        grid=(t // tm,),
        in_specs=[row(d), row(w), row(w), row(w), row(3 * d), _resident(wb.shape),
                  _resident(wo.shape), _resident((1, d))],
        out_specs=row(d),
        out_shape=jax.ShapeDtypeStruct((t, d), F32),
        compiler_params=_params("parallel"),
        name="merge",
    )(x, sb, cv, gl, gates, wb, wo, post.reshape(1, d))


def _mixer(x, b, s, pre, post, w_in, conv_w, conv_b, conv_ln_g, conv_ln_b,
           gla_w_alpha, gla_b_alpha, gla_norm_g, w_branch, w_out):
    t, d = x.shape
    w = BRANCH_WIDTH
    dk = GLA_KEY_DIM
    o_conv = 3 * w
    o_gla = o_conv + 2 * w
    o_lr = o_gla + 2 * dk + 2 * w
    o_gate = o_lr + GLA_GATE_RANK
    wqkv = w_in[:, :o_conv].astype(BF16)
    wconv = w_in[:, o_conv:o_gla].astype(BF16)
    wgla = w_in[:, o_gla:o_lr].astype(BF16)
    wlr = jnp.pad(w_in[:, o_lr:o_gate], ((0, 0), (0, LANES - GLA_GATE_RANK))).astype(BF16)
    wgate = w_in[:, o_gate:].astype(BF16)
    walpha = jnp.pad(gla_w_alpha, ((0, LANES - GLA_GATE_RANK), (0, 0)))

    q, k, v, u, gq, gk, gv, gr, la, gates = _inproj(
        x, pre, wqkv, wconv, wgla, wlr, wgate, walpha, gla_b_alpha)

    def seq(a):
        return a.reshape(b, s, a.shape[1])

    sb = _attn(seq(q), seq(k), seq(v)).reshape(t, w)
    cv = _conv(seq(u), conv_w, conv_b, conv_ln_g, conv_ln_b).reshape(t, w)
    gl = _gla(seq(gq), seq(gk), seq(gv), seq(gr), seq(la), gla_norm_g).reshape(t, w)
    return _merge(x, sb, cv, gl, gates, w_branch.astype(BF16), w_out.astype(BF16), post)


def kernel(x, norm_pre, norm_post, ffn1_w_gate, ffn1_w_up, ffn1_w_down, ffn2_w_gate, ffn2_w_up,
           ffn2_w_down, w_in, conv_w, conv_b, conv_ln_g, conv_ln_b, gla_w_alpha, gla_b_alpha,
           gla_norm_g, w_branch, w_out):
    b, s, d = x.shape
    xf = x.reshape(b * s, d)
    for l in range(norm_pre.shape[0]):
        xf = _ffn(xf, norm_pre[l, 0], norm_post[l, 0], ffn1_w_gate[l].astype(BF16),
                  ffn1_w_up[l].astype(BF16), ffn1_w_down[l].astype(BF16))
        xf = _mixer(xf, b, s, norm_pre[l, 1], norm_post[l, 1], w_in[l], conv_w[l], conv_b[l],
                    conv_ln_g[l], conv_ln_b[l], gla_w_alpha[l], gla_b_alpha[l], gla_norm_g[l],
                    w_branch[l], w_out[l])
        xf = _ffn(xf, norm_pre[l, 2], norm_post[l, 2], ffn2_w_gate[l].astype(BF16),
                  ffn2_w_up[l].astype(BF16), ffn2_w_down[l].astype(BF16))
    return xf.reshape(b, s, d)
```

```python
import functools

import jax
import jax.numpy as jnp
from jax import lax
from jax.experimental import pallas as pl
from jax.experimental.pallas import tpu as pltpu

NORM_EPS = 1e-6
BRANCH_WIDTH = 512
SB_HEAD_DIM = 64
CONV_WIDTH = 31
GLA_KEY_DIM = 256
GLA_HEADS = 4
GLA_HEAD_K = 64
GLA_HEAD_V = 128
GLA_GATE_RANK = 16
GLA_GATE_TAU = 16.0
GLA_CHUNK = 64

DEAD_STICK = 110.0

LANES = 128
SUBLANES = 8
VMEM_LIMIT_BYTES = 56 * 1024 * 1024

F32 = jnp.float32
BF16 = jnp.bfloat16


def _dot(a, b):
    return jnp.dot(a, b, preferred_element_type=F32)


def _dot_nt(a, b):
    return lax.dot_general(a, b, (((1,), (1,)), ((), ())), preferred_element_type=F32)


def _dot_tn(a, b, precision=None):
    return lax.dot_general(a, b, (((0,), (0,)), ((), ())), preferred_element_type=F32,
                           precision=precision)


def _rms(x, g):
    return x * lax.rsqrt(jnp.mean(x * x, axis=-1, keepdims=True) + NORM_EPS) * g


def _sigmoid(x):
    return 1.0 / (1.0 + jnp.exp(-x))


def _softplus(z):
    return jnp.maximum(z, 0.0) + jnp.log(1.0 + jnp.exp(-jnp.abs(z)))


def _params(*sem):
    return pltpu.CompilerParams(dimension_semantics=sem, vmem_limit_bytes=VMEM_LIMIT_BYTES)


def _resident(shape):
    nd = len(shape)
    return pl.BlockSpec(shape, lambda *_: (0,) * nd, pipeline_mode=pl.Buffered(1))


def _ffn_body(x_ref, pre_ref, post_ref, wg_ref, wu_ref, wd_ref, o_ref, acc_ref, *, fc):
    x = x_ref[...]
    h = _rms(x, pre_ref[...]).astype(BF16)
    d_ff = wg_ref.shape[1]
    for c0 in range(0, d_ff, fc):
        c1 = min(c0 + fc, d_ff)
        g = _dot(h, wg_ref[:, c0:c1])
        u = _dot(h, wu_ref[:, c0:c1])
        a = (g * _sigmoid(g) * u).astype(BF16)
        d = _dot(a, wd_ref[c0:c1, :])
        if c0 == 0:
            acc_ref[...] = d
        else:
            acc_ref[...] += d
    o_ref[...] = x + 0.5 * _rms(acc_ref[...], post_ref[...])


def _ffn(x, pre, post, wg, wu, wd, *, tm=512, fc=512):
    t, d = x.shape
    d_ff = wg.shape[1]
    row = pl.BlockSpec((tm, d), lambda i: (i, 0))
    return pl.pallas_call(
        functools.partial(_ffn_body, fc=fc),
        grid=(t // tm,),
        in_specs=[row, _resident((1, d)), _resident((1, d)),
                  _resident((d, d_ff)), _resident((d, d_ff)), _resident((d_ff, d))],
        out_specs=row,
        out_shape=jax.ShapeDtypeStruct((t, d), F32),
        scratch_shapes=[pltpu.VMEM((tm, d), F32)],
        compiler_params=_params("parallel"),
        name="ffn",
    )(x, pre.reshape(1, d), post.reshape(1, d), wg, wu, wd)


def _inproj_body(x_ref, pre_ref, wqkv_ref, wconv_ref, wgla_ref, wlr_ref, wgate_ref,
                 walpha_ref, balpha_ref,
                 q_ref, k_ref, v_ref, u_ref, gq_ref, gk_ref, gv_ref, gr_ref, la_ref, gates_ref):
    w = BRANCH_WIDTH
    dk = GLA_KEY_DIM
    h = _rms(x_ref[...], pre_ref[...]).astype(BF16)

    qkv = _dot(h, wqkv_ref[...])
    q_ref[...] = (qkv[:, :w] * (SB_HEAD_DIM ** -0.5)).astype(BF16)
    k_ref[...] = qkv[:, w:2 * w].astype(BF16)
    v_ref[...] = qkv[:, 2 * w:].astype(BF16)

    c = _dot(h, wconv_ref[...])
    u_ref[...] = (c[:, :w] * _sigmoid(c[:, w:])).astype(BF16)

    g4 = _dot(h, wgla_ref[...])
    gq_ref[...] = (g4[:, :dk] * (GLA_HEAD_K ** -0.5)).astype(BF16)
    gk_ref[...] = g4[:, dk:2 * dk].astype(BF16)
    gv_ref[...] = g4[:, 2 * dk:2 * dk + w].astype(BF16)
    gr_ref[...] = g4[:, 2 * dk + w:].astype(BF16)

    lr = _dot(h, wlr_ref[...])
    a = jnp.dot(lr, walpha_ref[...], preferred_element_type=F32,
                precision=lax.Precision.HIGHEST) + balpha_ref[...]
    la_ref[...] = (jnp.minimum(a, 0.0) - jnp.log(1.0 + jnp.exp(-jnp.abs(a)))) * (1.0 / GLA_GATE_TAU)

    gates_ref[...] = _sigmoid(_dot(h, wgate_ref[...])).astype(BF16)


def _inproj(x, pre, wqkv, wconv, wgla, wlr, wgate, walpha, balpha, *, tm=512):
    t, d = x.shape
    w = BRANCH_WIDTH
    dk = GLA_KEY_DIM
    n_gate = wgate.shape[1]

    def row(n):
        return pl.BlockSpec((tm, n), lambda i: (i, 0))

    def out(n, dt=BF16):
        return jax.ShapeDtypeStruct((t, n), dt)

    return pl.pallas_call(
        _inproj_body,
        grid=(t // tm,),
        in_specs=[row(d), _resident((1, d)), _resident(wqkv.shape), _resident(wconv.shape),
                  _resident(wgla.shape), _resident(wlr.shape), _resident(wgate.shape),
                  _resident(walpha.shape), _resident((1, dk))],
        out_specs=[row(w), row(w), row(w), row(w), row(dk), row(dk), row(w), row(w), row(dk),
                   row(n_gate)],
        out_shape=[out(w), out(w), out(w), out(w), out(dk), out(dk), out(w), out(w),
                   out(dk, F32), out(n_gate)],
        compiler_params=_params("parallel"),
        name="inproj",
    )(x, pre.reshape(1, d), wqkv, wconv, wgla, wlr, wgate, walpha, balpha.reshape(1, dk))


def _attn_body(q_ref, k_ref, v_ref, o_ref, stick_ref, acc_ref, *, tb, window):
    i = pl.program_id(1)
    n_pairs = q_ref.shape[2] // LANES
    rows = 2 * tb
    wk = window * tb
    lane = lax.broadcasted_iota(jnp.int32, (1, LANES), 1)
    head0 = lane < SB_HEAD_DIM

    def pair(ref, r0, n, p):
        return ref[0, pl.ds(r0, n), p * LANES:(p + 1) * LANES]

    def stacked_q(p):
        q = q_ref[0, :, p * LANES:(p + 1) * LANES]
        zero = jnp.zeros_like(q)
        return jnp.concatenate([jnp.where(head0, q, zero), jnp.where(head0, zero, q)], axis=0)

    def log_terms(z):
        lg = jnp.log(1.0 + jnp.exp(-jnp.abs(z)))
        return jnp.maximum(z, 0.0) + lg, jnp.minimum(z, 0.0) - lg

    def write_out(accs):
        for p in range(n_pairs):
            o_ref[0, :, p * LANES:(p + 1) * LANES] = jnp.where(
                head0, accs[p][:tb], accs[p][tb:]).astype(BF16)

    def tri(n):
        return jnp.where(lax.broadcasted_iota(jnp.int32, (n, n), 0) > lax.broadcasted_iota(jnp.int32, (n, n), 1),
                         1.0, 0.0)

    def query_row(n):
        r = lax.broadcasted_iota(jnp.int32, (rows, n), 0)
        return jnp.where(r >= tb, r - tb, r)

    def window_pass(w0, aligned):
        if aligned:
            keep = lax.broadcasted_iota(jnp.int32, (rows, tb), 1) < query_row(tb)

            def mask(a):
                return jnp.concatenate([a[:, :wk - tb], jnp.where(keep, a[:, wk - tb:], 0.0)], axis=1)
        else:
            keep = lax.broadcasted_iota(jnp.int32, (rows, wk), 1) < query_row(wk) + i * tb

            def mask(a):
                return jnp.where(keep, a, 0.0)

        sps, lss = [], []
        for p in range(n_pairs):
            sp, ls = log_terms(_dot_nt(stacked_q(p), pair(k_ref, w0, wk, p)))
            sps.append(mask(sp))
            lss.append(ls)
        later = _dot(jnp.concatenate(sps, axis=0).astype(BF16), tri(wk).astype(BF16))
        accs, sticks = [], []
        for p in range(n_pairs):
            lt = later[p * rows:(p + 1) * rows]
            wgt = mask(jnp.exp(lss[p] - lt))
            accs.append(_dot(wgt.astype(BF16), pair(v_ref, w0, wk, p)))
            sticks.append(lt[:, 0:1] + sps[p][:, 0:1])
        return accs, sticks

    def block_step(j0, p, stick, acc, suffix):
        sp, ls = log_terms(_dot_nt(stacked_q(p), pair(k_ref, j0, tb, p)))
        sums = _dot(sp.astype(BF16), suffix)
        wgt = jnp.exp(ls - sums[:, :tb] - stick)
        return stick + sums[:, tb:], acc + _dot(wgt.astype(BF16), pair(v_ref, j0, tb, p))

    def min_stick(sticks):
        m = sticks[0]
        for s in sticks[1:]:
            m = jnp.minimum(m, s)
        return jnp.min(m)

    @pl.when(i >= window - 1)
    def _():
        accs, sticks = window_pass(pl.multiple_of((i - (window - 1)) * tb, tb), True)
        write_out(accs)

        @pl.when(jnp.logical_and(i >= window, min_stick(sticks) < DEAD_STICK))
        def _():
            for p in range(n_pairs):
                stick_ref[p] = jnp.broadcast_to(sticks[p], (rows, tb))
                acc_ref[p] = accs[p]
            suffix = jnp.concatenate([tri(tb), jnp.ones((tb, tb), F32)], axis=1).astype(BF16)

            def cond(st):
                return jnp.logical_and(st[0] >= 0, st[1] > 0)

            def body(st):
                j0 = pl.multiple_of(st[0] * tb, tb)
                new = []
                for p in range(n_pairs):
                    stick, acc = block_step(j0, p, stick_ref[p], acc_ref[p], suffix)
                    stick_ref[p] = stick
                    acc_ref[p] = acc
                    new.append(stick)
                return st[0] - 1, (min_stick(new) < DEAD_STICK).astype(jnp.int32)

            lax.while_loop(cond, body, (i - window, jnp.int32(1)))
            write_out([acc_ref[p] for p in range(n_pairs)])

    @pl.when(i < window - 1)
    def _():
        accs, _ = window_pass(0, False)
        write_out(accs)


def _attn(q, k, v, *, tb=128, window=3):
    b, s, w = q.shape
    assert s % tb == 0 and s >= window * tb and w % LANES == 0
    n_pairs = w // LANES
    qspec = pl.BlockSpec((1, tb, w), lambda bi, i: (bi, i, 0))
    kvspec = pl.BlockSpec((1, s, w), lambda bi, i: (bi, 0, 0))
    return pl.pallas_call(
        functools.partial(_attn_body, tb=tb, window=window),
        grid=(b, s // tb),
        in_specs=[qspec, kvspec, kvspec],
        out_specs=qspec,
        out_shape=jax.ShapeDtypeStruct((b, s, w), BF16),
        scratch_shapes=[pltpu.VMEM((n_pairs, 2 * tb, tb), F32), pltpu.VMEM((n_pairs, 2 * tb, LANES), F32)],
        compiler_params=_params("parallel", "parallel"),
        name="stickbreak_attn",
    )(q, k, v)


def _conv_body(u_ref, halo_ref, shift_ref, w_ref, b_ref, g_ref, beta_ref, o_ref, *, tm, rb, halo):
    i = pl.program_id(1)
    prev = halo_ref[0]
    prev = jnp.where(i > 0, prev, jnp.zeros_like(prev))
    ext = jnp.concatenate([prev, u_ref[0]], axis=0)
    ne = rb + halo
    w = w_ref[...]
    bias = b_ref[...]
    gain = g_ref[...]
    beta = beta_ref[...]
    shift = shift_ref[...]
    first = halo - (CONV_WIDTH - 1)
    for r0 in range(0, tm, rb):
        shifted = _dot(shift, ext[r0:r0 + ne])
        acc = bias
        for j in range(CONV_WIDTH):
            b = (first + j) % SUBLANES
            start = b * ne + (first + j - b)
            acc = acc + w[j:j + 1, :] * shifted[start:start + rb]
        mu = jnp.mean(acc, axis=-1, keepdims=True)
        yc = acc - mu
        var = jnp.mean(yc * yc, axis=-1, keepdims=True)
        yn = yc * lax.rsqrt(var + NORM_EPS) * gain + beta
        o_ref[0, r0:r0 + rb, :] = (yn * _sigmoid(yn)).astype(BF16)


def _conv(u, w, bias, gain, beta, *, tm=512, rb=128, halo=32):
    b, s, ch = u.shape
    assert halo >= CONV_WIDTH - 1 and halo % SUBLANES == 0 and tm % halo == 0
    per = tm // halo
    ne = rb + halo
    m = jnp.arange(ne)
    shift = jnp.concatenate([(m[None, :] == m[:, None] + k) for k in range(SUBLANES)], axis=0).astype(BF16)
    vec = _resident((1, ch))
    return pl.pallas_call(
        functools.partial(_conv_body, tm=tm, rb=rb, halo=halo),
        grid=(b, s // tm),
        in_specs=[pl.BlockSpec((1, tm, ch), lambda bi, i: (bi, i, 0)),
                  pl.BlockSpec((1, halo, ch), lambda bi, i: (bi, jnp.maximum(i * per - 1, 0), 0)),
                  _resident(shift.shape), _resident((CONV_WIDTH, ch)), vec, vec, vec],
        out_specs=pl.BlockSpec((1, tm, ch), lambda bi, i: (bi, i, 0)),
        out_shape=jax.ShapeDtypeStruct((b, s, ch), BF16),
        compiler_params=_params("parallel", "parallel"),
        name="conv_module",
    )(u, u, shift, w, bias.reshape(1, ch), gain.reshape(1, ch), beta.reshape(1, ch))


def _gla_body(q_ref, k_ref, v_ref, r_ref, la_ref, g_ref, o_ref, state_ref, *, tm):
    ch = GLA_CHUNK
    dk = GLA_KEY_DIM

    @pl.when(pl.program_id(1) == 0)
    def _():
        state_ref[...] = jnp.zeros_like(state_ref)

    rr = lax.broadcasted_iota(jnp.int32, (ch, ch), 0)
    cc = lax.broadcasted_iota(jnp.int32, (ch, ch), 1)
    later = jnp.where(cc > rr, 1.0, 0.0).astype(BF16)
    key_head = lax.broadcasted_iota(jnp.int32, (1, dk), 1) // GLA_HEAD_K
    gain = g_ref[...]

    state = state_ref[...]
    for n in range(tm // ch):
        rows = slice(n * ch, (n + 1) * ch)
        la = la_ref[0, rows, :]
        la_hi = la.astype(BF16)
        la_lo = (la - la_hi.astype(F32)).astype(BF16)
        decay_to_end = _dot(later, la_hi) + _dot(later, la_lo)
        chunk_decay = jnp.exp(decay_to_end[0:1, :] + la[0:1, :])
        k_dec = (k_ref[0, rows, :].astype(F32) * jnp.exp(decay_to_end)).astype(BF16)
        state = state * chunk_decay + _dot_tn(v_ref[0, rows, :], k_dec)
        st = state.astype(BF16)
        q = q_ref[0, rows, :]
        normed = []
        for h in range(GLA_HEADS):
            qh = jnp.where(key_head == h, q, jnp.zeros_like(q))
            oh = _dot_nt(qh, st[h * GLA_HEAD_V:(h + 1) * GLA_HEAD_V, :])
            normed.append(oh * lax.rsqrt(jnp.mean(oh * oh, axis=-1, keepdims=True) + NORM_EPS))
        rg = r_ref[0, rows, :].astype(F32)
        o_ref[0, rows, :] = (jnp.concatenate(normed, axis=1) * gain * (rg * _sigmoid(rg))).astype(BF16)
    state_ref[...] = state


def _gla(q, k, v, r, la, gain, *, tm=512):
    b, s, dk = q.shape
    dv = v.shape[2]

    def blk(n):
        return pl.BlockSpec((1, tm, n), lambda bi, i: (bi, i, 0))

    return pl.pallas_call(
        functools.partial(_gla_body, tm=tm),
        grid=(b, s // tm),
        in_specs=[blk(dk), blk(dk), blk(dv), blk(dv), blk(dk), _resident((1, dv))],
        out_specs=blk(dv),
        out_shape=jax.ShapeDtypeStruct((b, s, dv), BF16),
        scratch_shapes=[pltpu.VMEM((dv, dk), F32)],
        compiler_params=_params("parallel", "arbitrary"),
        name="gla",
    )(q, k, v, r, la, gain.reshape(1, dv))


def _merge_body(x_ref, sb_ref, cv_ref, gl_ref, gates_ref, wb_ref, wo_ref, post_ref, o_ref):
    d = x_ref.shape[1]
    merged = gates_ref[:, 0:d].astype(F32) * _dot(sb_ref[...], wb_ref[0])
    merged = merged + gates_ref[:, d:2 * d].astype(F32) * _dot(cv_ref[...], wb_ref[1])
    merged = merged + gates_ref[:, 2 * d:3 * d].astype(F32) * _dot(gl_ref[...], wb_ref[2])
    m = _dot(merged.astype(BF16), wo_ref[...])
    o_ref[...] = x_ref[...] + _rms(m, post_ref[...])


def _merge(x, sb, cv, gl, gates, wb, wo, post, *, tm=512):
    t, d = x.shape
    w = sb.shape[1]

    def row(n):
        return pl.BlockSpec((tm, n), lambda i: (i, 0))

    return pl.pallas_call(
        _merge_body,
        grid=(t // tm,),
        in_specs=[row(d), row(w), row(w), row(w), row(3 * d), _resident(wb.shape),
                  _resident(wo.shape), _resident((1, d))],
        out_specs=row(d),
        out_shape=jax.ShapeDtypeStruct((t, d), F32),
        compiler_params=_params("parallel"),
        name="merge",
    )(x, sb, cv, gl, gates, wb, wo, post.reshape(1, d))


def _mixer(x, b, s, pre, post, w_in, conv_w, conv_b, conv_ln_g, conv_ln_b,
           gla_w_alpha, gla_b_alpha, gla_norm_g, w_branch, w_out):
    t, d = x.shape
    w = BRANCH_WIDTH
    dk = GLA_KEY_DIM
    o_conv = 3 * w
    o_gla = o_conv + 2 * w
    o_lr = o_gla + 2 * dk + 2 * w
    o_gate = o_lr + GLA_GATE_RANK
    wqkv = w_in[:, :o_conv].astype(BF16)
    wconv = w_in[:, o_conv:o_gla].astype(BF16)
    wgla = w_in[:, o_gla:o_lr].astype(BF16)
    wlr = jnp.pad(w_in[:, o_lr:o_gate], ((0, 0), (0, LANES - GLA_GATE_RANK))).astype(BF16)
    wgate = w_in[:, o_gate:].astype(BF16)
    walpha = jnp.pad(gla_w_alpha, ((0, LANES - GLA_GATE_RANK), (0, 0)))

    q, k, v, u, gq, gk, gv, gr, la, gates = _inproj(
        x, pre, wqkv, wconv, wgla, wlr, wgate, walpha, gla_b_alpha)

    def seq(a):
        return a.reshape(b, s, a.shape[1])

    sb = _attn(seq(q), seq(k), seq(v)).reshape(t, w)
    cv = _conv(seq(u), conv_w, conv_b, conv_ln_g, conv_ln_b).reshape(t, w)
    gl = _gla(seq(gq), seq(gk), seq(gv), seq(gr), seq(la), gla_norm_g).reshape(t, w)
    return _merge(x, sb, cv, gl, gates, w_branch.astype(BF16), w_out.astype(BF16), post)


def kernel(x, norm_pre, norm_post, ffn1_w_gate, ffn1_w_up, ffn1_w_down, ffn2_w_gate, ffn2_w_up,
           ffn2_w_down, w_in, conv_w, conv_b, conv_ln_g, conv_ln_b, gla_w_alpha, gla_b_alpha,
           gla_norm_g, w_branch, w_out):
    b, s, d = x.shape
    xf = x.reshape(b * s, d)
    for l in range(norm_pre.shape[0]):
        xf = _ffn(xf, norm_pre[l, 0], norm_post[l, 0], ffn1_w_gate[l].astype(BF16),
                  ffn1_w_up[l].astype(BF16), ffn1_w_down[l].astype(BF16))
        xf = _mixer(xf, b, s, norm_pre[l, 1], norm_post[l, 1], w_in[l], conv_w[l], conv_b[l],
                    conv_ln_g[l], conv_ln_b[l], gla_w_alpha[l], gla_b_alpha[l], gla_norm_g[l],
                    w_branch[l], w_out[l])
        xf = _ffn(xf, norm_pre[l, 2], norm_post[l, 2], ffn2_w_gate[l].astype(BF16),
                  ffn2_w_up[l].astype(BF16), ffn2_w_down[l].astype(BF16))
    return xf.reshape(b, s, d)
```

```python
import functools

import jax
import jax.numpy as jnp
from jax import lax
from jax.experimental import pallas as pl
from jax.experimental.pallas import tpu as pltpu

NORM_EPS = 1e-6
BRANCH_WIDTH = 512
SB_HEAD_DIM = 64
CONV_WIDTH = 31
GLA_KEY_DIM = 256
GLA_HEADS = 4
GLA_HEAD_K = 64
GLA_HEAD_V = 128
GLA_GATE_RANK = 16
GLA_GATE_TAU = 16.0
GLA_CHUNK = 64
GATE_CHUNK = 256

DEAD_STICK = 110.0

LANES = 128
SUBLANES = 8
VMEM_LIMIT_BYTES = 56 * 1024 * 1024

F32 = jnp.float32
BF16 = jnp.bfloat16


def _dot(a, b):
    return jnp.dot(a, b, preferred_element_type=F32)


def _dot_nt(a, b):
    return lax.dot_general(a, b, (((1,), (1,)), ((), ())), preferred_element_type=F32)


def _dot_tn(a, b):
    return lax.dot_general(a, b, (((0,), (0,)), ((), ())), preferred_element_type=F32)


def _rms(x, g):
    return x * lax.rsqrt(jnp.mean(x * x, axis=-1, keepdims=True) + NORM_EPS) * g


def _sigmoid(x):
    return 1.0 / (1.0 + jnp.exp(-x))


def _params(*sem):
    return pltpu.CompilerParams(dimension_semantics=sem, vmem_limit_bytes=VMEM_LIMIT_BYTES)


def _resident(shape):
    nd = len(shape)
    return pl.BlockSpec(shape, lambda *_: (0,) * nd, pipeline_mode=pl.Buffered(1))


def _ffn_body(x_ref, pre_ref, post_ref, wg_ref, wu_ref, wd_ref, o_ref, acc_ref, *, fc):
    x = x_ref[...]
    h = _rms(x, pre_ref[...]).astype(BF16)
    d_ff = wg_ref.shape[1]
    for c0 in range(0, d_ff, fc):
        c1 = min(c0 + fc, d_ff)
        g = _dot(h, wg_ref[:, c0:c1])
        u = _dot(h, wu_ref[:, c0:c1])
        a = (g * _sigmoid(g) * u).astype(BF16)
        d = _dot(a, wd_ref[c0:c1, :])
        if c0 == 0:
            acc_ref[...] = d
        else:
            acc_ref[...] += d
    o_ref[...] = x + 0.5 * _rms(acc_ref[...], post_ref[...])


def _ffn(x, pre, post, wg, wu, wd, *, tm=512, fc=512):
    t, d = x.shape
    d_ff = wg.shape[1]
    row = pl.BlockSpec((tm, d), lambda i: (i, 0))
    return pl.pallas_call(
        functools.partial(_ffn_body, fc=fc),
        grid=(t // tm,),
        in_specs=[row, _resident((1, d)), _resident((1, d)),
                  _resident((d, d_ff)), _resident((d, d_ff)), _resident((d_ff, d))],
        out_specs=row,
        out_shape=jax.ShapeDtypeStruct((t, d), F32),
        scratch_shapes=[pltpu.VMEM((tm, d), F32)],
        compiler_params=_params("parallel"),
        name="ffn",
    )(x, pre.reshape(1, d), post.reshape(1, d), wg, wu, wd)


def _conv_tile(ext_ref, shift, w, bias, gain, beta, o_ref, *, tm, rb, halo):
    ne = rb + halo
    first = halo - (CONV_WIDTH - 1)

    def block(r0):
        shifted = _dot(shift, ext_ref[r0:r0 + ne, :])
        acc = bias
        for j in range(CONV_WIDTH):
            b = (first + j) % SUBLANES
            start = b * ne + (first + j - b)
            acc = acc + w[j:j + 1, :] * shifted[start:start + rb]
        mu = jnp.mean(acc, axis=-1, keepdims=True)
        yc = acc - mu
        var = jnp.mean(yc * yc, axis=-1, keepdims=True)
        yn = yc * lax.rsqrt(var + NORM_EPS) * gain + beta
        o_ref[0, r0:r0 + rb, :] = (yn * _sigmoid(yn)).astype(BF16)

    return [functools.partial(block, r0) for r0 in range(0, tm, rb)]


def _gla_tile(q, k, v, r, la, gain, state_ref, o_ref, *, tm):
    ch = GLA_CHUNK
    rr = lax.broadcasted_iota(jnp.int32, (ch, ch), 0)
    cc = lax.broadcasted_iota(jnp.int32, (ch, ch), 1)
    later = jnp.where(cc > rr, 1.0, 0.0).astype(BF16)
    key_head = lax.broadcasted_iota(jnp.int32, (1, GLA_KEY_DIM), 1) // GLA_HEAD_K

    n_chunks = tm // ch
    carry = [state_ref[...]]

    def chunk(n):
        state = carry[0]
        rows = slice(n * ch, (n + 1) * ch)
        la_c = la[rows]
        la_hi = la_c.astype(BF16)
        la_lo = (la_c - la_hi.astype(F32)).astype(BF16)
        decay_to_end = _dot(later, la_hi) + _dot(later, la_lo)
        chunk_decay = jnp.exp(decay_to_end[0:1, :] + la_c[0:1, :])
        k_dec = (k[rows].astype(F32) * jnp.exp(decay_to_end)).astype(BF16)
        state = state * chunk_decay + _dot_tn(v[rows], k_dec)
        st = state.astype(BF16)
        q_c = q[rows]
        normed = []
        for h in range(GLA_HEADS):
            qh = jnp.where(key_head == h, q_c, jnp.zeros_like(q_c))
            oh = _dot_nt(qh, st[h * GLA_HEAD_V:(h + 1) * GLA_HEAD_V, :])
            normed.append(oh * lax.rsqrt(jnp.mean(oh * oh, axis=-1, keepdims=True) + NORM_EPS))
        rg = r[rows].astype(F32)
        o_ref[0, rows, :] = (jnp.concatenate(normed, axis=1) * gain * (rg * _sigmoid(rg))).astype(BF16)
        carry[0] = state
        if n == n_chunks - 1:
            state_ref[...] = state

    return [functools.partial(chunk, n) for n in range(n_chunks)]


def _mixer_in_body(x_ref, pre_ref, wqkv_ref, wconv_ref, wgla_ref, wlr_ref, wgate_ref,
                   walpha_ref, balpha_ref, shift_ref, cw_ref, cb_ref, cg_ref, cbeta_ref, gg_ref,
                   q_ref, k_ref, v_ref, gates_ref, cv_ref, gl_ref,
                   ext_ref, state_ref, *, tm, rb, halo):
    w = BRANCH_WIDTH
    dk = GLA_KEY_DIM

    @pl.when(pl.program_id(1) == 0)
    def _():
        ext_ref[0:halo, :] = jnp.zeros((halo, w), BF16)
        state_ref[...] = jnp.zeros_like(state_ref)

    @pl.when(pl.program_id(1) > 0)
    def _():
        ext_ref[0:halo, :] = ext_ref[tm:tm + halo, :]

    h = _rms(x_ref[0], pre_ref[...]).astype(BF16)

    qkv = _dot(h, wqkv_ref[...])
    q_ref[0] = (qkv[:, :w] * (SB_HEAD_DIM ** -0.5)).astype(BF16)
    k_ref[0] = qkv[:, w:2 * w].astype(BF16)
    v_ref[0] = qkv[:, 2 * w:].astype(BF16)

    c = _dot(h, wconv_ref[...])
    ext_ref[halo:halo + tm, :] = (c[:, :w] * _sigmoid(c[:, w:])).astype(BF16)
    tasks = _conv_tile(ext_ref, shift_ref[...], cw_ref[...], cb_ref[...], cg_ref[...], cbeta_ref[...], cv_ref,
                       tm=tm, rb=rb, halo=halo)

    g4 = _dot(h, wgla_ref[...])
    lr = _dot(h, wlr_ref[...])
    a = jnp.dot(lr, walpha_ref[...], preferred_element_type=F32,
                precision=lax.Precision.HIGHEST) + balpha_ref[...]
    la = (jnp.minimum(a, 0.0) - jnp.log(1.0 + jnp.exp(-jnp.abs(a)))) * (1.0 / GLA_GATE_TAU)
    tasks += _gla_tile((g4[:, :dk] * (GLA_HEAD_K ** -0.5)).astype(BF16), g4[:, dk:2 * dk].astype(BF16),
                       g4[:, 2 * dk:2 * dk + w].astype(BF16), g4[:, 2 * dk + w:].astype(BF16), la,
                       gg_ref[...], state_ref, gl_ref, tm=tm)

    n_gate = gates_ref.shape[2]
    gate_cols = [(c0, min(c0 + GATE_CHUNK, n_gate)) for c0 in range(0, n_gate, GATE_CHUNK)]
    per = -(-len(tasks) // len(gate_cols))
    for j, (c0, c1) in enumerate(gate_cols):
        gates_ref[0, :, c0:c1] = _sigmoid(_dot(h, wgate_ref[:, c0:c1])).astype(BF16)
        for task in tasks[j * per:(j + 1) * per]:
            task()


def _mixer_in(x, pre, wqkv, wconv, wgla, wlr, wgate, walpha, balpha, conv_w, conv_b, conv_g,
              conv_beta, gla_g, *, tm=512, rb=128, halo=32):
    b, s, d = x.shape
    w = BRANCH_WIDTH
    dk = GLA_KEY_DIM
    n_gate = wgate.shape[1]
    assert halo >= CONV_WIDTH - 1 and halo % (2 * SUBLANES) == 0 and tm % rb == 0 and s % tm == 0
    ne = rb + halo
    m = jnp.arange(ne)
    shift = jnp.concatenate([(m[None, :] == m[:, None] + j) for j in range(SUBLANES)], axis=0).astype(BF16)

    def row(n):
        return pl.BlockSpec((1, tm, n), lambda bi, i: (bi, i, 0))

    def out(n):
        return jax.ShapeDtypeStruct((b, s, n), BF16)

    vec = _resident((1, w))
    return pl.pallas_call(
        functools.partial(_mixer_in_body, tm=tm, rb=rb, halo=halo),
        grid=(b, s // tm),
        in_specs=[row(d), _resident((1, d)), _resident(wqkv.shape), _resident(wconv.shape),
                  _resident(wgla.shape), _resident(wlr.shape), _resident(wgate.shape),
                  _resident(walpha.shape), _resident((1, dk)), _resident(shift.shape),
                  _resident(conv_w.shape), vec, vec, vec, vec],
        out_specs=[row(w), row(w), row(w), row(n_gate), row(w), row(w)],
        out_shape=[out(w), out(w), out(w), out(n_gate), out(w), out(w)],
        scratch_shapes=[pltpu.VMEM((halo + tm, w), BF16), pltpu.VMEM((w, dk), F32)],
        compiler_params=_params("parallel", "arbitrary"),
        name="mixer_in",
    )(x, pre.reshape(1, d), wqkv, wconv, wgla, wlr, wgate, walpha, balpha.reshape(1, dk), shift,
      conv_w, conv_b.reshape(1, w), conv_g.reshape(1, w), conv_beta.reshape(1, w), gla_g.reshape(1, w))


def _attn_body(q_ref, k_ref, v_ref, o_ref, stick_ref, acc_ref, *, tb, window):
    i = pl.program_id(1)
    n_pairs = q_ref.shape[2] // LANES
    rows = 2 * tb
    wk = window * tb
    lane = lax.broadcasted_iota(jnp.int32, (1, LANES), 1)
    head0 = lane < SB_HEAD_DIM

    def pair(ref, r0, n, p):
        return ref[0, pl.ds(r0, n), p * LANES:(p + 1) * LANES]

    def stacked_q(p):
        q = q_ref[0, :, p * LANES:(p + 1) * LANES]
        zero = jnp.zeros_like(q)
        return jnp.concatenate([jnp.where(head0, q, zero), jnp.where(head0, zero, q)], axis=0)

    def log_terms(z):
        lg = jnp.log(1.0 + jnp.exp(-jnp.abs(z)))
        return jnp.maximum(z, 0.0) + lg, jnp.minimum(z, 0.0) - lg

    def write_out(accs):
        for p in range(n_pairs):
            o_ref[0, :, p * LANES:(p + 1) * LANES] = jnp.where(
                head0, accs[p][:tb], accs[p][tb:]).astype(BF16)

    def tri(n):
        return jnp.where(lax.broadcasted_iota(jnp.int32, (n, n), 0) > lax.broadcasted_iota(jnp.int32, (n, n), 1),
                         1.0, 0.0)

    def query_row(n):
        r = lax.broadcasted_iota(jnp.int32, (rows, n), 0)
        return jnp.where(r >= tb, r - tb, r)

    def window_pass(w0, aligned):
        if aligned:
            keep = lax.broadcasted_iota(jnp.int32, (rows, tb), 1) < query_row(tb)

            def mask(a):
                return jnp.concatenate([a[:, :wk - tb], jnp.where(keep, a[:, wk - tb:], 0.0)], axis=1)
        else:
            keep = lax.broadcasted_iota(jnp.int32, (rows, wk), 1) < query_row(wk) + i * tb

            def mask(a):
                return jnp.where(keep, a, 0.0)

        sps, lss = [], []
        for p in range(n_pairs):
            sp, ls = log_terms(_dot_nt(stacked_q(p), pair(k_ref, w0, wk, p)))
            sps.append(mask(sp))
            lss.append(ls)
        later = _dot(jnp.concatenate(sps, axis=0).astype(BF16), tri(wk).astype(BF16))
        accs, sticks = [], []
        for p in range(n_pairs):
            lt = later[p * rows:(p + 1) * rows]
            wgt = mask(jnp.exp(lss[p] - lt))
            accs.append(_dot(wgt.astype(BF16), pair(v_ref, w0, wk, p)))
            sticks.append(lt[:, 0:1] + sps[p][:, 0:1])
        return accs, sticks

    def block_step(j0, p, stick, acc, suffix):
        sp, ls = log_terms(_dot_nt(stacked_q(p), pair(k_ref, j0, tb, p)))
        sums = _dot(sp.astype(BF16), suffix)
        wgt = jnp.exp(ls - sums[:, :tb] - stick)
        return stick + sums[:, tb:], acc + _dot(wgt.astype(BF16), pair(v_ref, j0, tb, p))

    def min_stick(sticks):
        m = sticks[0]
        for s in sticks[1:]:
            m = jnp.minimum(m, s)
        return jnp.min(m)

    @pl.when(i >= window - 1)
    def _():
        accs, sticks = window_pass(pl.multiple_of((i - (window - 1)) * tb, tb), True)
        write_out(accs)

        @pl.when(jnp.logical_and(i >= window, min_stick(sticks) < DEAD_STICK))
        def _():
            for p in range(n_pairs):
                stick_ref[p] = jnp.broadcast_to(sticks[p], (rows, tb))
                acc_ref[p] = accs[p]
            suffix = jnp.concatenate([tri(tb), jnp.ones((tb, tb), F32)], axis=1).astype(BF16)

            def cond(st):
                return jnp.logical_and(st[0] >= 0, st[1] > 0)

            def body(st):
                j0 = pl.multiple_of(st[0] * tb, tb)
                new = []
                for p in range(n_pairs):
                    stick, acc = block_step(j0, p, stick_ref[p], acc_ref[p], suffix)
                    stick_ref[p] = stick
                    acc_ref[p] = acc
                    new.append(stick)
                return st[0] - 1, (min_stick(new) < DEAD_STICK).astype(jnp.int32)

            lax.while_loop(cond, body, (i - window, jnp.int32(1)))
            write_out([acc_ref[p] for p in range(n_pairs)])

    @pl.when(i < window - 1)
    def _():
        accs, _ = window_pass(0, False)
        write_out(accs)


def _attn(q, k, v, *, tb=128, window=3):
    b, s, w = q.shape
    assert s % tb == 0 and s >= window * tb and w % LANES == 0
    n_pairs = w // LANES
    qspec = pl.BlockSpec((1, tb, w), lambda bi, i: (bi, i, 0))
    kvspec = pl.BlockSpec((1, s, w), lambda bi, i: (bi, 0, 0))
    return pl.pallas_call(
        functools.partial(_attn_body, tb=tb, window=window),
        grid=(b, s // tb),
        in_specs=[qspec, kvspec, kvspec],
        out_specs=qspec,
        out_shape=jax.ShapeDtypeStruct((b, s, w), BF16),
        scratch_shapes=[pltpu.VMEM((n_pairs, 2 * tb, tb), F32), pltpu.VMEM((n_pairs, 2 * tb, LANES), F32)],
        compiler_params=_params("parallel", "parallel"),
        name="stickbreak_attn",
    )(q, k, v)


def _merge_body(x_ref, sb_ref, cv_ref, gl_ref, gates_ref, wb_ref, wo_ref, post_ref, o_ref):
    d = x_ref.shape[1]
    merged = gates_ref[:, 0:d].astype(F32) * _dot(sb_ref[...], wb_ref[0])
    merged = merged + gates_ref[:, d:2 * d].astype(F32) * _dot(cv_ref[...], wb_ref[1])
    merged = merged + gates_ref[:, 2 * d:3 * d].astype(F32) * _dot(gl_ref[...], wb_ref[2])
    m = _dot(merged.astype(BF16), wo_ref[...])
    o_ref[...] = x_ref[...] + _rms(m, post_ref[...])


def _merge(x, sb, cv, gl, gates, wb, wo, post, *, tm=512):
    t, d = x.shape
    w = sb.shape[1]

    def row(n):
        return pl.BlockSpec((tm, n), lambda i: (i, 0))

    return pl.pallas_call(
        _merge_body,
        grid=(t // tm,),
        in_specs=[row(d), row(w), row(w), row(w), row(3 * d), _resident(wb.shape),
                  _resident(wo.shape), _resident((1, d))],
        out_specs=row(d),
        out_shape=jax.ShapeDtypeStruct((t, d), F32),
        compiler_params=_params("parallel"),
        name="merge",
    )(x, sb, cv, gl, gates, wb, wo, post.reshape(1, d))


def _mixer(x, b, s, pre, post, w_in, conv_w, conv_b, conv_ln_g, conv_ln_b,
           gla_w_alpha, gla_b_alpha, gla_norm_g, w_branch, w_out):
    t, d = x.shape
    w = BRANCH_WIDTH
    dk = GLA_KEY_DIM
    o_conv = 3 * w
    o_gla = o_conv + 2 * w
    o_lr = o_gla + 2 * dk + 2 * w
    o_gate = o_lr + GLA_GATE_RANK
    wqkv = w_in[:, :o_conv].astype(BF16)
    wconv = w_in[:, o_conv:o_gla].astype(BF16)
    wgla = w_in[:, o_gla:o_lr].astype(BF16)
    wlr = jnp.pad(w_in[:, o_lr:o_gate], ((0, 0), (0, LANES - GLA_GATE_RANK))).astype(BF16)
    wgate = w_in[:, o_gate:].astype(BF16)
    walpha = jnp.pad(gla_w_alpha, ((0, LANES - GLA_GATE_RANK), (0, 0)))

    q, k, v, gates, cv, gl = _mixer_in(
        x.reshape(b, s, d), pre, wqkv, wconv, wgla, wlr, wgate, walpha, gla_b_alpha,
        conv_w, conv_b, conv_ln_g, conv_ln_b, gla_norm_g)
    sb = _attn(q, k, v)

    def flat(a):
        return a.reshape(t, a.shape[2])

    return _merge(x, flat(sb), flat(cv), flat(gl), flat(gates), w_branch.astype(BF16),
                  w_out.astype(BF16), post)


def kernel(x, norm_pre, norm_post, ffn1_w_gate, ffn1_w_up, ffn1_w_down, ffn2_w_gate, ffn2_w_up,
           ffn2_w_down, w_in, conv_w, conv_b, conv_ln_g, conv_ln_b, gla_w_alpha, gla_b_alpha,
           gla_norm_g, w_branch, w_out):
    b, s, d = x.shape
    xf = x.reshape(b * s, d)
    for l in range(norm_pre.shape[0]):
        xf = _ffn(xf, norm_pre[l, 0], norm_post[l, 0], ffn1_w_gate[l].astype(BF16),
                  ffn1_w_up[l].astype(BF16), ffn1_w_down[l].astype(BF16))
        xf = _mixer(xf, b, s, norm_pre[l, 1], norm_post[l, 1], w_in[l], conv_w[l], conv_b[l],
                    conv_ln_g[l], conv_ln_b[l], gla_w_alpha[l], gla_b_alpha[l], gla_norm_g[l],
                    w_branch[l], w_out[l])
        xf = _ffn(xf, norm_pre[l, 2], norm_post[l, 2], ffn2_w_gate[l].astype(BF16),
                  ffn2_w_up[l].astype(BF16), ffn2_w_down[l].astype(BF16))
    return xf.reshape(b, s, d)
```

```python
import functools

import jax
import jax.numpy as jnp
from jax import lax
from jax.experimental import pallas as pl
from jax.experimental.pallas import tpu as pltpu

NORM_EPS = 1e-6
BRANCH_WIDTH = 512
SB_HEAD_DIM = 64
CONV_WIDTH = 31
GLA_KEY_DIM = 256
GLA_HEADS = 4
GLA_HEAD_K = 64
GLA_HEAD_V = 128
GLA_GATE_RANK = 16
GLA_GATE_TAU = 16.0
GLA_CHUNK = 64
GATE_CHUNK = 256

DEAD_STICK = 110.0

LANES = 128
SUBLANES = 8
VMEM_LIMIT_BYTES = 56 * 1024 * 1024

F32 = jnp.float32
BF16 = jnp.bfloat16


def _dot(a, b):
    return jnp.dot(a, b, preferred_element_type=F32)


def _dot_nt(a, b):
    return lax.dot_general(a, b, (((1,), (1,)), ((), ())), preferred_element_type=F32)


def _dot_tn(a, b):
    return lax.dot_general(a, b, (((0,), (0,)), ((), ())), preferred_element_type=F32)


def _rms(x, g):
    return x * lax.rsqrt(jnp.mean(x * x, axis=-1, keepdims=True) + NORM_EPS) * g


def _sigmoid(x):
    return 1.0 / (1.0 + jnp.exp(-x))


def _params(*sem):
    return pltpu.CompilerParams(dimension_semantics=sem, vmem_limit_bytes=VMEM_LIMIT_BYTES)


def _resident(shape):
    nd = len(shape)
    return pl.BlockSpec(shape, lambda *_: (0,) * nd, pipeline_mode=pl.Buffered(1))


def _ffn_tile(x, pre, post, wg_ref, wu_ref, wd_ref, acc_ref, fc):
    h = _rms(x, pre).astype(BF16)
    d_ff = wg_ref.shape[1]
    for c0 in range(0, d_ff, fc):
        c1 = min(c0 + fc, d_ff)
        g = _dot(h, wg_ref[:, c0:c1])
        u = _dot(h, wu_ref[:, c0:c1])
        a = (g * _sigmoid(g) * u).astype(BF16)
        d = _dot(a, wd_ref[c0:c1, :])
        if c0 == 0:
            acc_ref[...] = d
        else:
            acc_ref[...] += d
    return x + 0.5 * _rms(acc_ref[...], post)


def _ffn_body(x_ref, pre_ref, post_ref, wg_ref, wu_ref, wd_ref, o_ref, acc_ref, *, fc):
    o_ref[...] = _ffn_tile(x_ref[...], pre_ref[...], post_ref[...], wg_ref, wu_ref, wd_ref, acc_ref, fc)


def _ffn(x, pre, post, wg, wu, wd, *, tm=512, fc=512):
    t, d = x.shape
    d_ff = wg.shape[1]
    row = pl.BlockSpec((tm, d), lambda i: (i, 0))
    return pl.pallas_call(
        functools.partial(_ffn_body, fc=fc),
        grid=(t // tm,),
        in_specs=[row, _resident((1, d)), _resident((1, d)),
                  _resident((d, d_ff)), _resident((d, d_ff)), _resident((d_ff, d))],
        out_specs=row,
        out_shape=jax.ShapeDtypeStruct((t, d), F32),
        scratch_shapes=[pltpu.VMEM((tm, d), F32)],
        compiler_params=_params("parallel"),
        name="ffn",
    )(x, pre.reshape(1, d), post.reshape(1, d), wg, wu, wd)


def _conv_tile(ext_ref, shift, w, bias, gain, beta, o_ref, *, tm, rb, halo):
    ne = rb + halo
    first = halo - (CONV_WIDTH - 1)

    def block(r0):
        shifted = _dot(shift, ext_ref[r0:r0 + ne, :])
        acc = bias
        for j in range(CONV_WIDTH):
            b = (first + j) % SUBLANES
            start = b * ne + (first + j - b)
            acc = acc + w[j:j + 1, :] * shifted[start:start + rb]
        mu = jnp.mean(acc, axis=-1, keepdims=True)
        yc = acc - mu
        var = jnp.mean(yc * yc, axis=-1, keepdims=True)
        yn = yc * lax.rsqrt(var + NORM_EPS) * gain + beta
        o_ref[0, r0:r0 + rb, :] = (yn * _sigmoid(yn)).astype(BF16)

    return [functools.partial(block, r0) for r0 in range(0, tm, rb)]


def _gla_tile(q, k, v, r, la, gain, state_ref, o_ref, *, tm):
    ch = GLA_CHUNK
    rr = lax.broadcasted_iota(jnp.int32, (ch, ch), 0)
    cc = lax.broadcasted_iota(jnp.int32, (ch, ch), 1)
    later = jnp.where(cc > rr, 1.0, 0.0).astype(BF16)
    key_head = lax.broadcasted_iota(jnp.int32, (1, GLA_KEY_DIM), 1) // GLA_HEAD_K

    n_chunks = tm // ch
    carry = [state_ref[...]]

    def chunk(n):
        state = carry[0]
        rows = slice(n * ch, (n + 1) * ch)
        la_c = la[rows]
        la_hi = la_c.astype(BF16)
        la_lo = (la_c - la_hi.astype(F32)).astype(BF16)
        decay_to_end = _dot(later, la_hi) + _dot(later, la_lo)
        chunk_decay = jnp.exp(decay_to_end[0:1, :] + la_c[0:1, :])
        k_dec = (k[rows].astype(F32) * jnp.exp(decay_to_end)).astype(BF16)
        state = state * chunk_decay + _dot_tn(v[rows], k_dec)
        st = state.astype(BF16)
        q_c = q[rows]
        normed = []
        for h in range(GLA_HEADS):
            qh = jnp.where(key_head == h, q_c, jnp.zeros_like(q_c))
            oh = _dot_nt(qh, st[h * GLA_HEAD_V:(h + 1) * GLA_HEAD_V, :])
            normed.append(oh * lax.rsqrt(jnp.mean(oh * oh, axis=-1, keepdims=True) + NORM_EPS))
        rg = r[rows].astype(F32)
        o_ref[0, rows, :] = (jnp.concatenate(normed, axis=1) * gain * (rg * _sigmoid(rg))).astype(BF16)
        carry[0] = state
        if n == n_chunks - 1:
            state_ref[...] = state

    return [functools.partial(chunk, n) for n in range(n_chunks)]


def _mixer_in_body(x_ref, pre_ref, wqkv_ref, wconv_ref, wgla_ref, wlr_ref, wgate_ref,
                   walpha_ref, balpha_ref, shift_ref, cw_ref, cb_ref, cg_ref, cbeta_ref, gg_ref,
                   q_ref, k_ref, v_ref, gates_ref, cv_ref, gl_ref,
                   ext_ref, state_ref, *, tm, rb, halo):
    w = BRANCH_WIDTH
    dk = GLA_KEY_DIM

    @pl.when(pl.program_id(1) == 0)
    def _():
        ext_ref[0:halo, :] = jnp.zeros((halo, w), BF16)
        state_ref[...] = jnp.zeros_like(state_ref)

    @pl.when(pl.program_id(1) > 0)
    def _():
        ext_ref[0:halo, :] = ext_ref[tm:tm + halo, :]

    h = _rms(x_ref[0], pre_ref[...]).astype(BF16)

    qkv = _dot(h, wqkv_ref[...])
    q_ref[0] = (qkv[:, :w] * (SB_HEAD_DIM ** -0.5)).astype(BF16)
    k_ref[0] = qkv[:, w:2 * w].astype(BF16)
    v_ref[0] = qkv[:, 2 * w:].astype(BF16)

    c = _dot(h, wconv_ref[...])
    ext_ref[halo:halo + tm, :] = (c[:, :w] * _sigmoid(c[:, w:])).astype(BF16)
    tasks = _conv_tile(ext_ref, shift_ref[...], cw_ref[...], cb_ref[...], cg_ref[...], cbeta_ref[...], cv_ref,
                       tm=tm, rb=rb, halo=halo)

    g4 = _dot(h, wgla_ref[...])
    lr = _dot(h, wlr_ref[...])
    a = jnp.dot(lr, walpha_ref[...], preferred_element_type=F32,
                precision=lax.Precision.HIGHEST) + balpha_ref[...]
    la = (jnp.minimum(a, 0.0) - jnp.log(1.0 + jnp.exp(-jnp.abs(a)))) * (1.0 / GLA_GATE_TAU)
    tasks += _gla_tile((g4[:, :dk] * (GLA_HEAD_K ** -0.5)).astype(BF16), g4[:, dk:2 * dk].astype(BF16),
                       g4[:, 2 * dk:2 * dk + w].astype(BF16), g4[:, 2 * dk + w:].astype(BF16), la,
                       gg_ref[...], state_ref, gl_ref, tm=tm)

    n_gate = gates_ref.shape[2]
    gate_cols = [(c0, min(c0 + GATE_CHUNK, n_gate)) for c0 in range(0, n_gate, GATE_CHUNK)]
    per = -(-len(tasks) // len(gate_cols))
    for j, (c0, c1) in enumerate(gate_cols):
        gates_ref[0, :, c0:c1] = _sigmoid(_dot(h, wgate_ref[:, c0:c1])).astype(BF16)
        for task in tasks[j * per:(j + 1) * per]:
            task()


def _mixer_in(x, pre, wqkv, wconv, wgla, wlr, wgate, walpha, balpha, conv_w, conv_b, conv_g,
              conv_beta, gla_g, *, tm=512, rb=128, halo=32):
    b, s, d = x.shape
    w = BRANCH_WIDTH
    dk = GLA_KEY_DIM
    n_gate = wgate.shape[1]
    assert halo >= CONV_WIDTH - 1 and halo % (2 * SUBLANES) == 0 and tm % rb == 0 and s % tm == 0
    ne = rb + halo
    m = jnp.arange(ne)
    shift = jnp.concatenate([(m[None, :] == m[:, None] + j) for j in range(SUBLANES)], axis=0).astype(BF16)

    def row(n):
        return pl.BlockSpec((1, tm, n), lambda bi, i: (bi, i, 0))

    def out(n):
        return jax.ShapeDtypeStruct((b, s, n), BF16)

    vec = _resident((1, w))
    return pl.pallas_call(
        functools.partial(_mixer_in_body, tm=tm, rb=rb, halo=halo),
        grid=(b, s // tm),
        in_specs=[row(d), _resident((1, d)), _resident(wqkv.shape), _resident(wconv.shape),
                  _resident(wgla.shape), _resident(wlr.shape), _resident(wgate.shape),
                  _resident(walpha.shape), _resident((1, dk)), _resident(shift.shape),
                  _resident(conv_w.shape), vec, vec, vec, vec],
        out_specs=[row(w), row(w), row(w), row(n_gate), row(w), row(w)],
        out_shape=[out(w), out(w), out(w), out(n_gate), out(w), out(w)],
        scratch_shapes=[pltpu.VMEM((halo + tm, w), BF16), pltpu.VMEM((w, dk), F32)],
        compiler_params=_params("parallel", "arbitrary"),
        name="mixer_in",
    )(x, pre.reshape(1, d), wqkv, wconv, wgla, wlr, wgate, walpha, balpha.reshape(1, dk), shift,
      conv_w, conv_b.reshape(1, w), conv_g.reshape(1, w), conv_beta.reshape(1, w), gla_g.reshape(1, w))


def _attn_body(q_ref, k_ref, v_ref, o_ref, stick_ref, acc_ref, *, tb, nq, window, band):
    i = pl.program_id(1)
    n_pairs = q_ref.shape[2] // LANES
    hb = tb // 2
    wk = window * tb
    lane = lax.broadcasted_iota(jnp.int32, (1, LANES), 1)
    head0 = lane < SB_HEAD_DIM

    def pair(ref, r0, n, p):
        return ref[0, pl.ds(r0, n), p * LANES:(p + 1) * LANES]

    def stacked_q(p, r0, n):
        q = pair(q_ref, r0, n, p)
        zero = jnp.zeros_like(q)
        return jnp.concatenate([jnp.where(head0, q, zero), jnp.where(head0, zero, q)], axis=0)

    def log_terms(z):
        lg = jnp.log(1.0 + jnp.exp(-jnp.abs(z)))
        return jnp.maximum(z, 0.0) + lg, jnp.minimum(z, 0.0) - lg

    def write_out(acc, r0, n, p):
        o_ref[0, pl.ds(r0, n), p * LANES:(p + 1) * LANES] = jnp.where(head0, acc[:n], acc[n:]).astype(BF16)

    def tri(n):
        return jnp.where(lax.broadcasted_iota(jnp.int32, (n, n), 0) > lax.broadcasted_iota(jnp.int32, (n, n), 1),
                         1.0, 0.0)

    def query_row(n_rows, n_cols):
        r = lax.broadcasted_iota(jnp.int32, (2 * n_rows, n_cols), 0)
        return jnp.where(r >= n_rows, r - n_rows, r)

    def masked_pass(qs_list, w0_list, width, mask):
        sps, lss = [], []
        for (p, qs), w0 in zip(qs_list, w0_list):
            sp, ls = log_terms(_dot_nt(qs, pair(k_ref, w0, width, p)))
            sps.append(mask(sp))
            lss.append(ls)
        later = _dot(jnp.concatenate(sps, axis=0).astype(BF16), tri(width).astype(BF16))
        n = later.shape[0] // len(sps)
        out = []
        for e, ((p, _), w0) in enumerate(zip(qs_list, w0_list)):
            lt = later[e * n:(e + 1) * n]
            wgt = mask(jnp.exp(lss[e] - lt))
            out.append((_dot(wgt.astype(BF16), pair(v_ref, w0, width, p)), lt[:, 0:1] + sps[e][:, 0:1]))
        return out

    def last_tile_mask(keep, width):
        def mask(a):
            return jnp.concatenate([a[:, :width - LANES], jnp.where(keep, a[:, width - LANES:], 0.0)], axis=1)
        return mask

    def band_pass():
        keep = lax.broadcasted_iota(jnp.int32, (2 * hb, LANES), 1) < query_row(hb, LANES) + (LANES - hb)
        entries, starts, rows0 = [], [], []
        for half in range(2 * nq):
            r0 = half * hb
            w0 = pl.multiple_of(i * (nq * tb) + (r0 + hb - band), hb)
            for p in range(n_pairs):
                entries.append((p, stacked_q(p, r0, hb)))
                starts.append(w0)
                rows0.append(r0)
        res = masked_pass(entries, starts, band, last_tile_mask(keep, band))
        m = res[0][1]
        for (acc, stick), (p, _), r0 in zip(res, entries, rows0):
            write_out(acc, r0, hb, p)
            m = jnp.minimum(m, stick)
        return jnp.min(m)

    def window_pass(blk_r0, w0, keep, aligned):
        if aligned:
            mask = last_tile_mask(keep, wk)
        else:
            def mask(a):
                return jnp.where(keep, a, 0.0)
        entries = [(p, stacked_q(p, blk_r0, tb)) for p in range(n_pairs)]
        res = masked_pass(entries, [w0] * n_pairs, wk, mask)
        for (acc, _), p in zip(res, range(n_pairs)):
            write_out(acc, blk_r0, tb, p)
        return res

    def block_step(j0, p, blk_r0, stick, acc, suffix):
        sp, ls = log_terms(_dot_nt(stacked_q(p, blk_r0, tb), pair(k_ref, j0, tb, p)))
        sums = _dot(sp.astype(BF16), suffix)
        wgt = jnp.exp(ls - sums[:, :tb] - stick)
        return stick + sums[:, tb:], acc + _dot(wgt.astype(BF16), pair(v_ref, j0, tb, p))

    def min_stick(sticks):
        m = sticks[0]
        for s in sticks[1:]:
            m = jnp.minimum(m, s)
        return jnp.min(m)

    def exact_block(blk, carry):
        blk_r0 = pl.multiple_of(blk * tb, tb)
        ib = i * nq + blk
        keep = lax.broadcasted_iota(jnp.int32, (2 * tb, LANES), 1) < query_row(tb, LANES) + (LANES - tb)
        res = window_pass(blk_r0, pl.multiple_of((ib - (window - 1)) * tb, tb), keep, True)
        for p in range(n_pairs):
            acc_ref[p] = res[p][0]
            stick_ref[p] = jnp.broadcast_to(res[p][1], (2 * tb, tb))
        suffix = jnp.concatenate([tri(tb), jnp.ones((tb, tb), F32)], axis=1).astype(BF16)

        def cond(st):
            return jnp.logical_and(st[0] >= 0, st[1] > 0)

        def body(st):
            j0 = pl.multiple_of(st[0] * tb, tb)
            new = []
            for p in range(n_pairs):
                stick, acc = block_step(j0, p, blk_r0, stick_ref[p], acc_ref[p], suffix)
                stick_ref[p] = stick
                acc_ref[p] = acc
                new.append(stick)
            return st[0] - 1, (min_stick(new) < DEAD_STICK).astype(jnp.int32)

        alive = (min_stick([r[1] for r in res]) < DEAD_STICK).astype(jnp.int32)
        lax.while_loop(cond, body, (ib - window, alive))
        for p in range(n_pairs):
            write_out(acc_ref[p], blk_r0, tb, p)
        return carry

    @pl.when(i > 0)
    def _():
        @pl.when(band_pass() < DEAD_STICK)
        def _():
            lax.fori_loop(0, nq, exact_block, 0)

    @pl.when(i == 0)
    def _():
        for blk in range(nq):
            keep = lax.broadcasted_iota(jnp.int32, (2 * tb, wk), 1) < query_row(tb, wk) + blk * tb
            window_pass(blk * tb, 0, keep, False)


def _attn(q, k, v, *, tb=128, nq=2, window=3, band=256):
    b, s, w = q.shape
    tq = nq * tb
    assert s % tq == 0 and w % LANES == 0 and tq <= window * tb <= s
    assert band % LANES == 0 and tb // 2 <= LANES and band - tb // 2 <= tq and (window - 1) * tb <= tq
    n_pairs = w // LANES
    qspec = pl.BlockSpec((1, tq, w), lambda bi, i: (bi, i, 0))
    kvspec = pl.BlockSpec((1, s, w), lambda bi, i: (bi, 0, 0))
    return pl.pallas_call(
        functools.partial(_attn_body, tb=tb, nq=nq, window=window, band=band),
        grid=(b, s // tq),
        in_specs=[qspec, kvspec, kvspec],
        out_specs=qspec,
        out_shape=jax.ShapeDtypeStruct((b, s, w), BF16),
        scratch_shapes=[pltpu.VMEM((n_pairs, 2 * tb, tb), F32), pltpu.VMEM((n_pairs, 2 * tb, LANES), F32)],
        compiler_params=_params("parallel", "parallel"),
        name="stickbreak_attn",
    )(q, k, v)


def _merge_ffn_body(x_ref, sb_ref, cv_ref, gl_ref, gates_ref, wb_ref, wo_ref, mpost_ref,
                    pre_ref, post_ref, wg_ref, wu_ref, wd_ref, o_ref, acc_ref, *, fc):
    d = x_ref.shape[1]
    merged = gates_ref[:, 0:d].astype(F32) * _dot(sb_ref[...], wb_ref[0])
    merged = merged + gates_ref[:, d:2 * d].astype(F32) * _dot(cv_ref[...], wb_ref[1])
    merged = merged + gates_ref[:, 2 * d:3 * d].astype(F32) * _dot(gl_ref[...], wb_ref[2])
    m = _dot(merged.astype(BF16), wo_ref[...])
    x = x_ref[...] + _rms(m, mpost_ref[...])
    o_ref[...] = _ffn_tile(x, pre_ref[...], post_ref[...], wg_ref, wu_ref, wd_ref, acc_ref, fc)


def _merge_ffn(x, sb, cv, gl, gates, wb, wo, mpost, pre, post, wg, wu, wd, *, tm=512, fc=512):
    t, d = x.shape
    w = sb.shape[1]
    d_ff = wg.shape[1]

    def row(n):
        return pl.BlockSpec((tm, n), lambda i: (i, 0))

    vec = _resident((1, d))
    return pl.pallas_call(
        functools.partial(_merge_ffn_body, fc=fc),
        grid=(t // tm,),
        in_specs=[row(d), row(w), row(w), row(w), row(3 * d), _resident(wb.shape),
                  _resident(wo.shape), vec, vec, vec,
                  _resident((d, d_ff)), _resident((d, d_ff)), _resident((d_ff, d))],
        out_specs=row(d),
        out_shape=jax.ShapeDtypeStruct((t, d), F32),
        scratch_shapes=[pltpu.VMEM((tm, d), F32)],
        compiler_params=_params("parallel"),
        name="merge_ffn",
    )(x, sb, cv, gl, gates, wb, wo, mpost.reshape(1, d), pre.reshape(1, d), post.reshape(1, d),
      wg, wu, wd)


def _mixer_and_ffn(x, b, s, pre, post, w_in, conv_w, conv_b, conv_ln_g, conv_ln_b,
                   gla_w_alpha, gla_b_alpha, gla_norm_g, w_branch, w_out,
                   ffn_pre, ffn_post, ffn_wg, ffn_wu, ffn_wd):
    t, d = x.shape
    w = BRANCH_WIDTH
    dk = GLA_KEY_DIM
    o_conv = 3 * w
    o_gla = o_conv + 2 * w
    o_lr = o_gla + 2 * dk + 2 * w
    o_gate = o_lr + GLA_GATE_RANK
    wqkv = w_in[:, :o_conv].astype(BF16)
    wconv = w_in[:, o_conv:o_gla].astype(BF16)
    wgla = w_in[:, o_gla:o_lr].astype(BF16)
    wlr = jnp.pad(w_in[:, o_lr:o_gate], ((0, 0), (0, LANES - GLA_GATE_RANK))).astype(BF16)
    wgate = w_in[:, o_gate:].astype(BF16)
    walpha = jnp.pad(gla_w_alpha, ((0, LANES - GLA_GATE_RANK), (0, 0)))

    q, k, v, gates, cv, gl = _mixer_in(
        x.reshape(b, s, d), pre, wqkv, wconv, wgla, wlr, wgate, walpha, gla_b_alpha,
        conv_w, conv_b, conv_ln_g, conv_ln_b, gla_norm_g)
    sb = _attn(q, k, v)

    def flat(a):
        return a.reshape(t, a.shape[2])

    return _merge_ffn(x, flat(sb), flat(cv), flat(gl), flat(gates), w_branch.astype(BF16),
                      w_out.astype(BF16), post, ffn_pre, ffn_post, ffn_wg.astype(BF16),
                      ffn_wu.astype(BF16), ffn_wd.astype(BF16))


def kernel(x, norm_pre, norm_post, ffn1_w_gate, ffn1_w_up, ffn1_w_down, ffn2_w_gate, ffn2_w_up,
           ffn2_w_down, w_in, conv_w, conv_b, conv_ln_g, conv_ln_b, gla_w_alpha, gla_b_alpha,
           gla_norm_g, w_branch, w_out):
    b, s, d = x.shape
    xf = x.reshape(b * s, d)
    for l in range(norm_pre.shape[0]):
        xf = _ffn(xf, norm_pre[l, 0], norm_post[l, 0], ffn1_w_gate[l].astype(BF16),
                  ffn1_w_up[l].astype(BF16), ffn1_w_down[l].astype(BF16))
        xf = _mixer_and_ffn(xf, b, s, norm_pre[l, 1], norm_post[l, 1], w_in[l], conv_w[l], conv_b[l],
                            conv_ln_g[l], conv_ln_b[l], gla_w_alpha[l], gla_b_alpha[l], gla_norm_g[l],
                            w_branch[l], w_out[l], norm_pre[l, 2], norm_post[l, 2],
                            ffn2_w_gate[l], ffn2_w_up[l], ffn2_w_down[l])
    return xf.reshape(b, s, d)
```

```python
import functools

import jax
import jax.numpy as jnp
from jax import lax
from jax.experimental import pallas as pl
from jax.experimental.pallas import tpu as pltpu

NORM_EPS = 1e-6
BRANCH_WIDTH = 512
SB_HEAD_DIM = 64
CONV_WIDTH = 31
GLA_KEY_DIM = 256
GLA_HEADS = 4
GLA_HEAD_K = 64
GLA_HEAD_V = 128
GLA_GATE_RANK = 16
GLA_GATE_TAU = 16.0
GLA_CHUNK = 64
N_BRANCHES = 3
GATE_CHUNK = 256

DEAD_STICK = 110.0

LANES = 128
SUBLANES = 8
VMEM_LIMIT_BYTES = 56 * 1024 * 1024

F32 = jnp.float32
BF16 = jnp.bfloat16


def _dot(a, b):
    return jnp.dot(a, b, preferred_element_type=F32)


def _dot_nt(a, b):
    return lax.dot_general(a, b, (((1,), (1,)), ((), ())), preferred_element_type=F32)


def _dot_tn(a, b):
    return lax.dot_general(a, b, (((0,), (0,)), ((), ())), preferred_element_type=F32)


def _rms(x, g):
    return x * lax.rsqrt(jnp.mean(x * x, axis=-1, keepdims=True) + NORM_EPS) * g


def _sigmoid(x):
    return 1.0 / (1.0 + jnp.exp(-x))


def _params(*sem):
    return pltpu.CompilerParams(dimension_semantics=sem, vmem_limit_bytes=VMEM_LIMIT_BYTES)


def _resident(shape):
    nd = len(shape)
    return pl.BlockSpec(shape, lambda *_: (0,) * nd, pipeline_mode=pl.Buffered(1))


def _resident_layer(stacked, layer):
    nd = stacked.ndim - 1
    return pl.BlockSpec((None,) + stacked.shape[1:], lambda *_: (layer,) + (0,) * nd,
                        pipeline_mode=pl.Buffered(1))


def _ffn_tile(x, pre, post, wg_ref, wu_ref, wd_ref, acc_ref, fc):
    h = _rms(x, pre).astype(BF16)
    d_ff = wg_ref.shape[1]
    for c0 in range(0, d_ff, fc):
        c1 = min(c0 + fc, d_ff)
        g = _dot(h, wg_ref[:, c0:c1])
        u = _dot(h, wu_ref[:, c0:c1])
        a = (g * _sigmoid(g) * u).astype(BF16)
        d = _dot(a, wd_ref[c0:c1, :])
        if c0 == 0:
            acc_ref[...] = d
        else:
            acc_ref[...] += d
    return x + 0.5 * _rms(acc_ref[...], post)


def _ffn_body(x_ref, pre_ref, post_ref, wg_ref, wu_ref, wd_ref, o_ref, acc_ref, *, fc):
    o_ref[...] = _ffn_tile(x_ref[...], pre_ref[...], post_ref[...], wg_ref, wu_ref, wd_ref, acc_ref, fc)


def _ffn(x, pre, post, wg, wu, wd, layer, *, tm=512, fc=512):
    t, d = x.shape
    row = pl.BlockSpec((tm, d), lambda i: (i, 0))
    return pl.pallas_call(
        functools.partial(_ffn_body, fc=fc),
        grid=(t // tm,),
        in_specs=[row, _resident((1, d)), _resident((1, d)),
                  _resident_layer(wg, layer), _resident_layer(wu, layer), _resident_layer(wd, layer)],
        out_specs=row,
        out_shape=jax.ShapeDtypeStruct((t, d), F32),
        scratch_shapes=[pltpu.VMEM((tm, d), F32)],
        compiler_params=_params("parallel"),
        name="ffn",
    )(x, pre.reshape(1, d), post.reshape(1, d), wg, wu, wd)


def _conv_tile(ext_ref, shift, w, bias, gain, beta, o_ref, *, tm, rb, halo):
    ne = rb + halo
    first = halo - (CONV_WIDTH - 1)

    def block(r0):
        shifted = _dot(shift, ext_ref[r0:r0 + ne, :])
        acc = bias
        for j in range(CONV_WIDTH):
            b = (first + j) % SUBLANES
            start = b * ne + (first + j - b)
            acc = acc + w[j:j + 1, :] * shifted[start:start + rb]
        mu = jnp.mean(acc, axis=-1, keepdims=True)
        yc = acc - mu
        var = jnp.mean(yc * yc, axis=-1, keepdims=True)
        yn = yc * lax.rsqrt(var + NORM_EPS) * gain + beta
        o_ref[0, r0:r0 + rb, :] = (yn * _sigmoid(yn)).astype(BF16)

    return [functools.partial(block, r0) for r0 in range(0, tm, rb)]


def _gla_tile(q, k, v, r, la, gain, state_ref, o_ref, *, tm):
    ch = GLA_CHUNK
    rr = lax.broadcasted_iota(jnp.int32, (ch, ch), 0)
    cc = lax.broadcasted_iota(jnp.int32, (ch, ch), 1)
    later = jnp.where(cc > rr, 1.0, 0.0).astype(BF16)
    key_head = lax.broadcasted_iota(jnp.int32, (1, GLA_KEY_DIM), 1) // GLA_HEAD_K

    n_chunks = tm // ch
    carry = [state_ref[...]]

    def chunk(n):
        state = carry[0]
        rows = slice(n * ch, (n + 1) * ch)
        la_c = la[rows]
        la_hi = la_c.astype(BF16)
        la_lo = (la_c - la_hi.astype(F32)).astype(BF16)
        decay_to_end = _dot(later, la_hi) + _dot(later, la_lo)
        chunk_decay = jnp.exp(decay_to_end[0:1, :] + la_c[0:1, :])
        k_dec = (k[rows].astype(F32) * jnp.exp(decay_to_end)).astype(BF16)
        state = state * chunk_decay + _dot_tn(v[rows], k_dec)
        st = state.astype(BF16)
        q_c = q[rows]
        normed = []
        for h in range(GLA_HEADS):
            qh = jnp.where(key_head == h, q_c, jnp.zeros_like(q_c))
            oh = _dot_nt(qh, st[h * GLA_HEAD_V:(h + 1) * GLA_HEAD_V, :])
            normed.append(oh * lax.rsqrt(jnp.mean(oh * oh, axis=-1, keepdims=True) + NORM_EPS))
        rg = r[rows].astype(F32)
        o_ref[0, rows, :] = (jnp.concatenate(normed, axis=1) * gain * (rg * _sigmoid(rg))).astype(BF16)
        carry[0] = state
        if n == n_chunks - 1:
            state_ref[...] = state

    return [functools.partial(chunk, n) for n in range(n_chunks)]


def _mixer_in_body(x_ref, pre_ref, win_ref, walpha_ref, balpha_ref, shift_ref, cw_ref, cb_ref, cg_ref, cbeta_ref, gg_ref,
                   q_ref, k_ref, v_ref, gates_ref, cv_ref, gl_ref,
                   ext_ref, state_ref, *, tm, rb, halo):
    w = BRANCH_WIDTH
    dk = GLA_KEY_DIM

    @pl.when(pl.program_id(1) == 0)
    def _():
        ext_ref[0:halo, :] = jnp.zeros((halo, w), BF16)
        state_ref[...] = jnp.zeros_like(state_ref)

    @pl.when(pl.program_id(1) > 0)
    def _():
        ext_ref[0:halo, :] = ext_ref[tm:tm + halo, :]

    h = _rms(x_ref[0], pre_ref[...]).astype(BF16)
    o_conv = 3 * w
    o_gla = o_conv + 2 * w
    o_lr = o_gla + 2 * dk + 2 * w
    o_gate = o_lr + LANES

    qkv = _dot(h, win_ref[:, :o_conv])
    q_ref[0] = (qkv[:, :w] * (SB_HEAD_DIM ** -0.5)).astype(BF16)
    k_ref[0] = qkv[:, w:2 * w].astype(BF16)
    v_ref[0] = qkv[:, 2 * w:].astype(BF16)

    c = _dot(h, win_ref[:, o_conv:o_gla])
    ext_ref[halo:halo + tm, :] = (c[:, :w] * _sigmoid(c[:, w:])).astype(BF16)
    tasks = _conv_tile(ext_ref, shift_ref[...], cw_ref[...], cb_ref[...], cg_ref[...], cbeta_ref[...], cv_ref,
                       tm=tm, rb=rb, halo=halo)

    g4 = _dot(h, win_ref[:, o_gla:o_lr])
    lr = _dot(h, win_ref[:, o_lr:o_gate])
    a = jnp.dot(lr, walpha_ref[...], preferred_element_type=F32,
                precision=lax.Precision.HIGHEST) + balpha_ref[...]
    la = (jnp.minimum(a, 0.0) - jnp.log(1.0 + jnp.exp(-jnp.abs(a)))) * (1.0 / GLA_GATE_TAU)
    tasks += _gla_tile((g4[:, :dk] * (GLA_HEAD_K ** -0.5)).astype(BF16), g4[:, dk:2 * dk].astype(BF16),
                       g4[:, 2 * dk:2 * dk + w].astype(BF16), g4[:, 2 * dk + w:].astype(BF16), la,
                       gg_ref[...], state_ref, gl_ref, tm=tm)

    n_gate = gates_ref.shape[2]
    gate_cols = [(c0, min(c0 + GATE_CHUNK, n_gate)) for c0 in range(0, n_gate, GATE_CHUNK)]
    per = -(-len(tasks) // len(gate_cols))
    for j, (c0, c1) in enumerate(gate_cols):
        gates_ref[0, :, c0:c1] = _sigmoid(_dot(h, win_ref[:, o_gate + c0:o_gate + c1])).astype(BF16)
        for task in tasks[j * per:(j + 1) * per]:
            task()


def _mixer_in(x, pre, w_in, layer, walpha, balpha, conv_w, conv_b, conv_g,
              conv_beta, gla_g, *, tm=512, rb=128, halo=32):
    b, s, d = x.shape
    w = BRANCH_WIDTH
    dk = GLA_KEY_DIM
    n_gate = N_BRANCHES * d
    assert w_in.shape[2] == 5 * w + 2 * dk + 2 * w + LANES + n_gate
    assert halo >= CONV_WIDTH - 1 and halo % (2 * SUBLANES) == 0 and tm % rb == 0 and s % tm == 0
    ne = rb + halo
    m = jnp.arange(ne)
    shift = jnp.concatenate([(m[None, :] == m[:, None] + j) for j in range(SUBLANES)], axis=0).astype(BF16)

    def row(n):
        return pl.BlockSpec((1, tm, n), lambda bi, i: (bi, i, 0))

    def out(n):
        return jax.ShapeDtypeStruct((b, s, n), BF16)

    vec = _resident((1, w))
    return pl.pallas_call(
        functools.partial(_mixer_in_body, tm=tm, rb=rb, halo=halo),
        grid=(b, s // tm),
        in_specs=[row(d), _resident((1, d)), _resident_layer(w_in, layer),
                  _resident(walpha.shape), _resident((1, dk)), _resident(shift.shape),
                  _resident(conv_w.shape), vec, vec, vec, vec],
        out_specs=[row(w), row(w), row(w), row(n_gate), row(w), row(w)],
        out_shape=[out(w), out(w), out(w), out(n_gate), out(w), out(w)],
        scratch_shapes=[pltpu.VMEM((halo + tm, w), BF16), pltpu.VMEM((w, dk), F32)],
        compiler_params=_params("parallel", "arbitrary"),
        name="mixer_in",
    )(x, pre.reshape(1, d), w_in, walpha, balpha.reshape(1, dk), shift,
      conv_w, conv_b.reshape(1, w), conv_g.reshape(1, w), conv_beta.reshape(1, w), gla_g.reshape(1, w))


def _attn_body(q_ref, k_ref, v_ref, o_ref, stick_ref, acc_ref, *, tb, nq, window, band):
    i = pl.program_id(1)
    n_pairs = q_ref.shape[2] // LANES
    hb = tb // 2
    wk = window * tb
    lane = lax.broadcasted_iota(jnp.int32, (1, LANES), 1)
    head0 = lane < SB_HEAD_DIM

    def pair(ref, r0, n, p):
        return ref[0, pl.ds(r0, n), p * LANES:(p + 1) * LANES]

    def stacked_q(p, r0, n):
        q = pair(q_ref, r0, n, p)
        zero = jnp.zeros_like(q)
        return jnp.concatenate([jnp.where(head0, q, zero), jnp.where(head0, zero, q)], axis=0)

    def log_terms(z):
        lg = jnp.log(1.0 + jnp.exp(-jnp.abs(z)))
        return jnp.maximum(z, 0.0) + lg, jnp.minimum(z, 0.0) - lg

    def write_out(acc, r0, n, p):
        o_ref[0, pl.ds(r0, n), p * LANES:(p + 1) * LANES] = jnp.where(head0, acc[:n], acc[n:]).astype(BF16)

    def tri(n):
        return jnp.where(lax.broadcasted_iota(jnp.int32, (n, n), 0) > lax.broadcasted_iota(jnp.int32, (n, n), 1),
                         1.0, 0.0)

    def query_row(n_rows, n_cols):
        r = lax.broadcasted_iota(jnp.int32, (2 * n_rows, n_cols), 0)
        return jnp.where(r >= n_rows, r - n_rows, r)

    def masked_pass(qs_list, w0_list, width, mask):
        sps, lss = [], []
        for (p, qs), w0 in zip(qs_list, w0_list):
            sp, ls = log_terms(_dot_nt(qs, pair(k_ref, w0, width, p)))
            sps.append(mask(sp))
            lss.append(ls)
        later = _dot(jnp.concatenate(sps, axis=0).astype(BF16), tri(width).astype(BF16))
        n = later.shape[0] // len(sps)
        out = []
        for e, ((p, _), w0) in enumerate(zip(qs_list, w0_list)):
            lt = later[e * n:(e + 1) * n]
            wgt = mask(jnp.exp(lss[e] - lt))
            out.append((_dot(wgt.astype(BF16), pair(v_ref, w0, width, p)), lt[:, 0:1] + sps[e][:, 0:1]))
        return out

    def last_tile_mask(keep, width):
        def mask(a):
            return jnp.concatenate([a[:, :width - LANES], jnp.where(keep, a[:, width - LANES:], 0.0)], axis=1)
        return mask

    def band_pass():
        keep = lax.broadcasted_iota(jnp.int32, (2 * hb, LANES), 1) < query_row(hb, LANES) + (LANES - hb)
        entries, starts, rows0 = [], [], []
        for half in range(2 * nq):
            r0 = half * hb
            w0 = pl.multiple_of(i * (nq * tb) + (r0 + hb - band), hb)
            for p in range(n_pairs):
                entries.append((p, stacked_q(p, r0, hb)))
                starts.append(w0)
                rows0.append(r0)
        res = masked_pass(entries, starts, band, last_tile_mask(keep, band))
        m = res[0][1]
        for (acc, stick), (p, _), r0 in zip(res, entries, rows0):
            write_out(acc, r0, hb, p)
            m = jnp.minimum(m, stick)
        return jnp.min(m)

    def window_pass(blk_r0, w0, keep, aligned):
        if aligned:
            mask = last_tile_mask(keep, wk)
        else:
            def mask(a):
                return jnp.where(keep, a, 0.0)
        entries = [(p, stacked_q(p, blk_r0, tb)) for p in range(n_pairs)]
        res = masked_pass(entries, [w0] * n_pairs, wk, mask)
        for (acc, _), p in zip(res, range(n_pairs)):
            write_out(acc, blk_r0, tb, p)
        return res

    def block_step(j0, p, blk_r0, stick, acc, suffix):
        sp, ls = log_terms(_dot_nt(stacked_q(p, blk_r0, tb), pair(k_ref, j0, tb, p)))
        sums = _dot(sp.astype(BF16), suffix)
        wgt = jnp.exp(ls - sums[:, :tb] - stick)
        return stick + sums[:, tb:], acc + _dot(wgt.astype(BF16), pair(v_ref, j0, tb, p))

    def min_stick(sticks):
        m = sticks[0]
        for s in sticks[1:]:
            m = jnp.minimum(m, s)
        return jnp.min(m)

    def exact_block(blk, carry):
        blk_r0 = pl.multiple_of(blk * tb, tb)
        ib = i * nq + blk
        keep = lax.broadcasted_iota(jnp.int32, (2 * tb, LANES), 1) < query_row(tb, LANES) + (LANES - tb)
        res = window_pass(blk_r0, pl.multiple_of((ib - (window - 1)) * tb, tb), keep, True)
        for p in range(n_pairs):
            acc_ref[p] = res[p][0]
            stick_ref[p] = jnp.broadcast_to(res[p][1], (2 * tb, tb))
        suffix = jnp.concatenate([tri(tb), jnp.ones((tb, tb), F32)], axis=1).astype(BF16)

        def cond(st):
            return jnp.logical_and(st[0] >= 0, st[1] > 0)

        def body(st):
            j0 = pl.multiple_of(st[0] * tb, tb)
            new = []
            for p in range(n_pairs):
                stick, acc = block_step(j0, p, blk_r0, stick_ref[p], acc_ref[p], suffix)
                stick_ref[p] = stick
                acc_ref[p] = acc
                new.append(stick)
            return st[0] - 1, (min_stick(new) < DEAD_STICK).astype(jnp.int32)

        alive = (min_stick([r[1] for r in res]) < DEAD_STICK).astype(jnp.int32)
        lax.while_loop(cond, body, (ib - window, alive))
        for p in range(n_pairs):
            write_out(acc_ref[p], blk_r0, tb, p)
        return carry

    @pl.when(i > 0)
    def _():
        @pl.when(band_pass() < DEAD_STICK)
        def _():
            lax.fori_loop(0, nq, exact_block, 0)

    @pl.when(i == 0)
    def _():
        for blk in range(nq):
            keep = lax.broadcasted_iota(jnp.int32, (2 * tb, wk), 1) < query_row(tb, wk) + blk * tb
            window_pass(blk * tb, 0, keep, False)


def _attn(q, k, v, *, tb=128, nq=2, window=3, band=256):
    b, s, w = q.shape
    tq = nq * tb
    assert s % tq == 0 and w % LANES == 0 and tq <= window * tb <= s
    assert band % LANES == 0 and tb // 2 <= LANES and band - tb // 2 <= tq and (window - 1) * tb <= tq
    n_pairs = w // LANES
    qspec = pl.BlockSpec((1, tq, w), lambda bi, i: (bi, i, 0))
    kvspec = pl.BlockSpec((1, s, w), lambda bi, i: (bi, 0, 0))
    return pl.pallas_call(
        functools.partial(_attn_body, tb=tb, nq=nq, window=window, band=band),
        grid=(b, s // tq),
        in_specs=[qspec, kvspec, kvspec],
        out_specs=qspec,
        out_shape=jax.ShapeDtypeStruct((b, s, w), BF16),
        scratch_shapes=[pltpu.VMEM((n_pairs, 2 * tb, tb), F32), pltpu.VMEM((n_pairs, 2 * tb, LANES), F32)],
        compiler_params=_params("parallel", "parallel"),
        name="stickbreak_attn",
    )(q, k, v)


def _merge_ffn_body(x_ref, sb_ref, cv_ref, gl_ref, gates_ref, wb_ref, wo_ref, mpost_ref,
                    pre_ref, post_ref, wg_ref, wu_ref, wd_ref, o_ref, acc_ref, *, fc):
    d = x_ref.shape[1]
    merged = gates_ref[:, 0:d].astype(F32) * _dot(sb_ref[...], wb_ref[0])
    merged = merged + gates_ref[:, d:2 * d].astype(F32) * _dot(cv_ref[...], wb_ref[1])
    merged = merged + gates_ref[:, 2 * d:3 * d].astype(F32) * _dot(gl_ref[...], wb_ref[2])
    m = _dot(merged.astype(BF16), wo_ref[...])
    x = x_ref[...] + _rms(m, mpost_ref[...])
    o_ref[...] = _ffn_tile(x, pre_ref[...], post_ref[...], wg_ref, wu_ref, wd_ref, acc_ref, fc)


def _merge_ffn(x, sb, cv, gl, gates, wb, wo, mpost, pre, post, wg, wu, wd, layer, *, tm=512, fc=512):
    t, d = x.shape
    w = sb.shape[1]

    def row(n):
        return pl.BlockSpec((tm, n), lambda i: (i, 0))

    vec = _resident((1, d))
    return pl.pallas_call(
        functools.partial(_merge_ffn_body, fc=fc),
        grid=(t // tm,),
        in_specs=[row(d), row(w), row(w), row(w), row(3 * d), _resident_layer(wb, layer),
                  _resident_layer(wo, layer), vec, vec, vec,
                  _resident_layer(wg, layer), _resident_layer(wu, layer), _resident_layer(wd, layer)],
        out_specs=row(d),
        out_shape=jax.ShapeDtypeStruct((t, d), F32),
        scratch_shapes=[pltpu.VMEM((tm, d), F32)],
        compiler_params=_params("parallel"),
        name="merge_ffn",
    )(x, sb, cv, gl, gates, wb, wo, mpost.reshape(1, d), pre.reshape(1, d), post.reshape(1, d),
      wg, wu, wd)


def kernel(x, norm_pre, norm_post, ffn1_w_gate, ffn1_w_up, ffn1_w_down, ffn2_w_gate, ffn2_w_up,
           ffn2_w_down, w_in, conv_w, conv_b, conv_ln_g, conv_ln_b, gla_w_alpha, gla_b_alpha,
           gla_norm_g, w_branch, w_out):
    b, s, d = x.shape
    t = b * s
    n_layers = norm_pre.shape[0]

    ffn1 = [a.astype(BF16) for a in (ffn1_w_gate, ffn1_w_up, ffn1_w_down)]
    ffn2 = [a.astype(BF16) for a in (ffn2_w_gate, ffn2_w_up, ffn2_w_down)]
    o_gate = w_in.shape[2] - N_BRANCHES * d
    lane_pad = jnp.zeros((n_layers, d, LANES - GLA_GATE_RANK), w_in.dtype)
    w_in_b = jnp.concatenate([w_in[:, :, :o_gate], lane_pad, w_in[:, :, o_gate:]], axis=2).astype(BF16)
    walpha = jnp.pad(gla_w_alpha, ((0, 0), (0, LANES - GLA_GATE_RANK), (0, 0)))
    w_branch_b = w_branch.astype(BF16)
    w_out_b = w_out.astype(BF16)

    def flat(a):
        return a.reshape(t, a.shape[2])

    xf = x.reshape(t, d)
    for l in range(n_layers):
        xf = _ffn(xf, norm_pre[l, 0], norm_post[l, 0], *ffn1, l)
        q, k, v, gates, cv, gl = _mixer_in(
            xf.reshape(b, s, d), norm_pre[l, 1], w_in_b, l, walpha[l], gla_b_alpha[l],
            conv_w[l], conv_b[l], conv_ln_g[l], conv_ln_b[l], gla_norm_g[l])
        sb = _attn(q, k, v)
        xf = _merge_ffn(xf, flat(sb), flat(cv), flat(gl), flat(gates), w_branch_b, w_out_b,
                        norm_post[l, 1], norm_pre[l, 2], norm_post[l, 2], *ffn2, l)
    return xf.reshape(b, s, d)
```

```python
import functools

import jax
import jax.numpy as jnp
from jax import lax
from jax.experimental import pallas as pl
from jax.experimental.pallas import tpu as pltpu

NORM_EPS = 1e-6
BRANCH_WIDTH = 512
SB_HEAD_DIM = 64
CONV_WIDTH = 31
GLA_KEY_DIM = 256
GLA_HEADS = 4
GLA_HEAD_K = 64
GLA_HEAD_V = 128
GLA_GATE_RANK = 16
GLA_GATE_TAU = 16.0
GLA_CHUNK = 64
N_BRANCHES = 3
GATE_CHUNK = 256

DEAD_STICK = 110.0

LANES = 128
SUBLANES = 8
VMEM_LIMIT_BYTES = 56 * 1024 * 1024

F32 = jnp.float32
BF16 = jnp.bfloat16


def _dot(a, b):
    return jnp.dot(a, b, preferred_element_type=F32)


def _dot_nt(a, b):
    return lax.dot_general(a, b, (((1,), (1,)), ((), ())), preferred_element_type=F32)


def _dot_tn(a, b):
    return lax.dot_general(a, b, (((0,), (0,)), ((), ())), preferred_element_type=F32)


def _rms(x, g):
    return x * lax.rsqrt(jnp.mean(x * x, axis=-1, keepdims=True) + NORM_EPS) * g


def _sigmoid(x):
    return 1.0 / (1.0 + jnp.exp(-x))


def _params(*sem):
    return pltpu.CompilerParams(dimension_semantics=sem, vmem_limit_bytes=VMEM_LIMIT_BYTES)


def _resident(shape):
    nd = len(shape)
    return pl.BlockSpec(shape, lambda *_: (0,) * nd, pipeline_mode=pl.Buffered(1))


def _resident_layer(stacked, layer):
    nd = stacked.ndim - 1
    return pl.BlockSpec((None,) + stacked.shape[1:], lambda *_: (layer,) + (0,) * nd,
                        pipeline_mode=pl.Buffered(1))


def _ffn_tile(x, pre, post, wg_ref, wu_ref, wd_ref, acc_ref, fc):
    h = _rms(x, pre).astype(BF16)
    d_ff = wg_ref.shape[1]
    for c0 in range(0, d_ff, fc):
        c1 = min(c0 + fc, d_ff)
        g = _dot(h, wg_ref[:, c0:c1])
        u = _dot(h, wu_ref[:, c0:c1])
        a = (g * _sigmoid(g) * u).astype(BF16)
        d = _dot(a, wd_ref[c0:c1, :])
        if c0 == 0:
            acc_ref[...] = d
        else:
            acc_ref[...] += d
    return x + 0.5 * _rms(acc_ref[...], post)


def _ffn_body(x_ref, pre_ref, post_ref, wg_ref, wu_ref, wd_ref, o_ref, acc_ref, *, fc):
    o_ref[...] = _ffn_tile(x_ref[...], pre_ref[...], post_ref[...], wg_ref, wu_ref, wd_ref, acc_ref, fc)


def _ffn(x, pre, post, wg, wu, wd, layer, *, tm=512, fc=512):
    t, d = x.shape
    row = pl.BlockSpec((tm, d), lambda i: (i, 0))
    return pl.pallas_call(
        functools.partial(_ffn_body, fc=fc),
        grid=(t // tm,),
        in_specs=[row, _resident((1, d)), _resident((1, d)),
                  _resident_layer(wg, layer), _resident_layer(wu, layer), _resident_layer(wd, layer)],
        out_specs=row,
        out_shape=jax.ShapeDtypeStruct((t, d), F32),
        scratch_shapes=[pltpu.VMEM((tm, d), F32)],
        compiler_params=_params("parallel"),
        name="ffn",
    )(x, pre.reshape(1, d), post.reshape(1, d), wg, wu, wd)


def _conv_tile(ext_ref, shift, w, bias, gain, beta, o_ref, *, tm, rb, halo):
    ne = rb + halo
    first = halo - (CONV_WIDTH - 1)

    def block(r0):
        shifted = _dot(shift, ext_ref[r0:r0 + ne, :])
        acc = bias
        for j in range(CONV_WIDTH):
            b = (first + j) % SUBLANES
            start = b * ne + (first + j - b)
            acc = acc + w[j:j + 1, :] * shifted[start:start + rb]
        mu = jnp.mean(acc, axis=-1, keepdims=True)
        yc = acc - mu
        var = jnp.mean(yc * yc, axis=-1, keepdims=True)
        yn = yc * lax.rsqrt(var + NORM_EPS) * gain + beta
        o_ref[0, r0:r0 + rb, :] = (yn * _sigmoid(yn)).astype(BF16)

    return [functools.partial(block, r0) for r0 in range(0, tm, rb)]


def _gla_tile(q, k, v, r, la, gain, state_ref, o_ref, *, tm):
    ch = GLA_CHUNK
    rr = lax.broadcasted_iota(jnp.int32, (ch, ch), 0)
    cc = lax.broadcasted_iota(jnp.int32, (ch, ch), 1)
    later = jnp.where(cc > rr, 1.0, 0.0).astype(BF16)
    key_head = lax.broadcasted_iota(jnp.int32, (1, GLA_KEY_DIM), 1) // GLA_HEAD_K

    n_chunks = tm // ch
    carry = [state_ref[...]]

    def chunk(n):
        state = carry[0]
        rows = slice(n * ch, (n + 1) * ch)
        la_c = la[rows]
        la_hi = la_c.astype(BF16)
        la_lo = (la_c - la_hi.astype(F32)).astype(BF16)
        decay_to_end = _dot(later, la_hi) + _dot(later, la_lo)
        chunk_decay = jnp.exp(decay_to_end[0:1, :] + la_c[0:1, :])
        k_dec = (k[rows].astype(F32) * jnp.exp(decay_to_end)).astype(BF16)
        state = state * chunk_decay + _dot_tn(v[rows], k_dec)
        st = state.astype(BF16)
        q_c = q[rows]
        normed = []
        for h in range(GLA_HEADS):
            qh = jnp.where(key_head == h, q_c, jnp.zeros_like(q_c))
            oh = _dot_nt(qh, st[h * GLA_HEAD_V:(h + 1) * GLA_HEAD_V, :])
            normed.append(oh * lax.rsqrt(jnp.mean(oh * oh, axis=-1, keepdims=True) + NORM_EPS))
        rg = r[rows].astype(F32)
        o_ref[0, rows, :] = (jnp.concatenate(normed, axis=1) * gain * (rg * _sigmoid(rg))).astype(BF16)
        carry[0] = state
        if n == n_chunks - 1:
            state_ref[...] = state

    return [functools.partial(chunk, n) for n in range(n_chunks)]


def _mixer_in_body(x_ref, pre_ref, win_ref, wlr_ref, wgate_ref, walpha_ref, balpha_ref,
                   shift_ref, cw_ref, cb_ref, cg_ref, cbeta_ref, gg_ref,
                   q_ref, k_ref, v_ref, gates_ref, cv_ref, gl_ref,
                   ext_ref, state_ref, *, tm, rb, halo):
    w = BRANCH_WIDTH
    dk = GLA_KEY_DIM

    @pl.when(pl.program_id(1) == 0)
    def _():
        ext_ref[0:halo, :] = jnp.zeros((halo, w), BF16)
        state_ref[...] = jnp.zeros_like(state_ref)

    @pl.when(pl.program_id(1) > 0)
    def _():
        ext_ref[0:halo, :] = ext_ref[tm:tm + halo, :]

    h = _rms(x_ref[0], pre_ref[...]).astype(BF16)
    o_conv = 3 * w
    o_gla = o_conv + 2 * w
    o_lr = o_gla + 2 * dk + 2 * w

    qkv = _dot(h, win_ref[:, :o_conv])
    q_ref[0] = (qkv[:, :w] * (SB_HEAD_DIM ** -0.5)).astype(BF16)
    k_ref[0] = qkv[:, w:2 * w].astype(BF16)
    v_ref[0] = qkv[:, 2 * w:].astype(BF16)

    c = _dot(h, win_ref[:, o_conv:o_gla])
    ext_ref[halo:halo + tm, :] = (c[:, :w] * _sigmoid(c[:, w:])).astype(BF16)
    tasks = _conv_tile(ext_ref, shift_ref[...], cw_ref[...], cb_ref[...], cg_ref[...], cbeta_ref[...], cv_ref,
                       tm=tm, rb=rb, halo=halo)

    g4 = _dot(h, win_ref[:, o_gla:o_lr])
    lr = _dot(h, wlr_ref[...])
    a = jnp.dot(lr, walpha_ref[...], preferred_element_type=F32,
                precision=lax.Precision.HIGHEST) + balpha_ref[...]
    la = (jnp.minimum(a, 0.0) - jnp.log(1.0 + jnp.exp(-jnp.abs(a)))) * (1.0 / GLA_GATE_TAU)
    tasks += _gla_tile((g4[:, :dk] * (GLA_HEAD_K ** -0.5)).astype(BF16), g4[:, dk:2 * dk].astype(BF16),
                       g4[:, 2 * dk:2 * dk + w].astype(BF16), g4[:, 2 * dk + w:].astype(BF16), la,
                       gg_ref[...], state_ref, gl_ref, tm=tm)

    n_gate = gates_ref.shape[2]
    gate_cols = [(c0, min(c0 + GATE_CHUNK, n_gate)) for c0 in range(0, n_gate, GATE_CHUNK)]
    per = -(-len(tasks) // len(gate_cols))
    for j, (c0, c1) in enumerate(gate_cols):
        gates_ref[0, :, c0:c1] = _sigmoid(_dot(h, wgate_ref[:, c0:c1])).astype(BF16)
        for task in tasks[j * per:(j + 1) * per]:
            task()


def _mixer_in(x, pre, w_main, w_lr, w_gate, layer, walpha, balpha, conv_w, conv_b, conv_g,
              conv_beta, gla_g, *, tm=512, rb=128, halo=32):
    b, s, d = x.shape
    w = BRANCH_WIDTH
    dk = GLA_KEY_DIM
    n_gate = w_gate.shape[2]
    assert w_main.shape[2] == 5 * w + 2 * dk + 2 * w and w_lr.shape[2] == LANES
    assert halo >= CONV_WIDTH - 1 and halo % (2 * SUBLANES) == 0 and tm % rb == 0 and s % tm == 0
    ne = rb + halo
    m = jnp.arange(ne)
    shift = jnp.concatenate([(m[None, :] == m[:, None] + j) for j in range(SUBLANES)], axis=0).astype(BF16)

    def row(n):
        return pl.BlockSpec((1, tm, n), lambda bi, i: (bi, i, 0))

    def out(n):
        return jax.ShapeDtypeStruct((b, s, n), BF16)

    vec = _resident((1, w))
    return pl.pallas_call(
        functools.partial(_mixer_in_body, tm=tm, rb=rb, halo=halo),
        grid=(b, s // tm),
        in_specs=[row(d), _resident((1, d)), _resident_layer(w_main, layer),
                  _resident_layer(w_lr, layer), _resident_layer(w_gate, layer),
                  _resident(walpha.shape), _resident((1, dk)), _resident(shift.shape),
                  _resident(conv_w.shape), vec, vec, vec, vec],
        out_specs=[row(w), row(w), row(w), row(n_gate), row(w), row(w)],
        out_shape=[out(w), out(w), out(w), out(n_gate), out(w), out(w)],
        scratch_shapes=[pltpu.VMEM((halo + tm, w), BF16), pltpu.VMEM((w, dk), F32)],
        compiler_params=_params("parallel", "arbitrary"),
        name="mixer_in",
    )(x, pre.reshape(1, d), w_main, w_lr, w_gate, walpha, balpha.reshape(1, dk), shift,
      conv_w, conv_b.reshape(1, w), conv_g.reshape(1, w), conv_beta.reshape(1, w), gla_g.reshape(1, w))


def _attn_body(q_ref, k_ref, v_ref, o_ref, stick_ref, acc_ref, *, tb, nq, window, band):
    i = pl.program_id(1)
    n_pairs = q_ref.shape[2] // LANES
    hb = tb // 2
    wk = window * tb
    lane = lax.broadcasted_iota(jnp.int32, (1, LANES), 1)
    head0 = lane < SB_HEAD_DIM

    def pair(ref, r0, n, p):
        return ref[0, pl.ds(r0, n), p * LANES:(p + 1) * LANES]

    def stacked_q(p, r0, n):
        q = pair(q_ref, r0, n, p)
        zero = jnp.zeros_like(q)
        return jnp.concatenate([jnp.where(head0, q, zero), jnp.where(head0, zero, q)], axis=0)

    def log_terms(z):
        lg = jnp.log(1.0 + jnp.exp(-jnp.abs(z)))
        return jnp.maximum(z, 0.0) + lg, jnp.minimum(z, 0.0) - lg

    def write_out(acc, r0, n, p):
        o_ref[0, pl.ds(r0, n), p * LANES:(p + 1) * LANES] = jnp.where(head0, acc[:n], acc[n:]).astype(BF16)

    def tri(n):
        return jnp.where(lax.broadcasted_iota(jnp.int32, (n, n), 0) > lax.broadcasted_iota(jnp.int32, (n, n), 1),
                         1.0, 0.0)

    def query_row(n_rows, n_cols):
        r = lax.broadcasted_iota(jnp.int32, (2 * n_rows, n_cols), 0)
        return jnp.where(r >= n_rows, r - n_rows, r)

    def masked_pass(qs_list, w0_list, width, mask):
        sps, lss = [], []
        for (p, qs), w0 in zip(qs_list, w0_list):
            sp, ls = log_terms(_dot_nt(qs, pair(k_ref, w0, width, p)))
            sps.append(mask(sp))
            lss.append(ls)
        later = _dot(jnp.concatenate(sps, axis=0).astype(BF16), tri(width).astype(BF16))
        n = later.shape[0] // len(sps)
        out = []
        for e, ((p, _), w0) in enumerate(zip(qs_list, w0_list)):
            lt = later[e * n:(e + 1) * n]
            wgt = mask(jnp.exp(lss[e] - lt))
            out.append((_dot(wgt.astype(BF16), pair(v_ref, w0, width, p)), lt[:, 0:1] + sps[e][:, 0:1]))
        return out

    def last_tile_mask(keep, width):
        def mask(a):
            return jnp.concatenate([a[:, :width - LANES], jnp.where(keep, a[:, width - LANES:], 0.0)], axis=1)
        return mask

    def band_pass():
        keep = lax.broadcasted_iota(jnp.int32, (2 * hb, LANES), 1) < query_row(hb, LANES) + (LANES - hb)
        entries, starts, rows0 = [], [], []
        for half in range(2 * nq):
            r0 = half * hb
            w0 = pl.multiple_of(i * (nq * tb) + (r0 + hb - band), hb)
            for p in range(n_pairs):
                entries.append((p, stacked_q(p, r0, hb)))
                starts.append(w0)
                rows0.append(r0)
        res = masked_pass(entries, starts, band, last_tile_mask(keep, band))
        mins = [None] * nq
        for (acc, stick), (p, _), r0 in zip(res, entries, rows0):
            write_out(acc, r0, hb, p)
            blk = r0 // tb
            mins[blk] = stick if mins[blk] is None else jnp.minimum(mins[blk], stick)
        return [jnp.min(m) for m in mins]

    def window_pass(blk_r0, w0, keep, aligned):
        if aligned:
            mask = last_tile_mask(keep, wk)
        else:
            def mask(a):
                return jnp.where(keep, a, 0.0)
        entries = [(p, stacked_q(p, blk_r0, tb)) for p in range(n_pairs)]
        res = masked_pass(entries, [w0] * n_pairs, wk, mask)
        for (acc, _), p in zip(res, range(n_pairs)):
            write_out(acc, blk_r0, tb, p)
        return res

    def block_step(j0, p, blk_r0, stick, acc, suffix):
        sp, ls = log_terms(_dot_nt(stacked_q(p, blk_r0, tb), pair(k_ref, j0, tb, p)))
        sums = _dot(sp.astype(BF16), suffix)
        wgt = jnp.exp(ls - sums[:, :tb] - stick)
        return stick + sums[:, tb:], acc + _dot(wgt.astype(BF16), pair(v_ref, j0, tb, p))

    def min_stick(sticks):
        m = sticks[0]
        for s in sticks[1:]:
            m = jnp.minimum(m, s)
        return jnp.min(m)

    def exact_block(blk):
        blk_r0 = blk * tb
        ib = i * nq + blk
        keep = lax.broadcasted_iota(jnp.int32, (2 * tb, LANES), 1) < query_row(tb, LANES) + (LANES - tb)
        res = window_pass(blk_r0, pl.multiple_of((ib - (window - 1)) * tb, tb), keep, True)
        for p in range(n_pairs):
            acc_ref[p] = res[p][0]
            stick_ref[p] = jnp.broadcast_to(res[p][1], (2 * tb, tb))
        suffix = jnp.concatenate([tri(tb), jnp.ones((tb, tb), F32)], axis=1).astype(BF16)

        def cond(st):
            return jnp.logical_and(st[0] >= 0, st[1] > 0)

        def body(st):
            j0 = pl.multiple_of(st[0] * tb, tb)
            new = []
            for p in range(n_pairs):
                stick, acc = block_step(j0, p, blk_r0, stick_ref[p], acc_ref[p], suffix)
                stick_ref[p] = stick
                acc_ref[p] = acc
                new.append(stick)
            return st[0] - 1, (min_stick(new) < DEAD_STICK).astype(jnp.int32)

        alive = (min_stick([r[1] for r in res]) < DEAD_STICK).astype(jnp.int32)
        lax.while_loop(cond, body, (ib - window, alive))
        for p in range(n_pairs):
            write_out(acc_ref[p], blk_r0, tb, p)

    @pl.when(i > 0)
    def _():
        for blk, lowest in enumerate(band_pass()):
            pl.when(lowest < DEAD_STICK)(functools.partial(exact_block, blk))

    @pl.when(i == 0)
    def _():
        for blk in range(nq):
            keep = lax.broadcasted_iota(jnp.int32, (2 * tb, wk), 1) < query_row(tb, wk) + blk * tb
            window_pass(blk * tb, 0, keep, False)


def _attn(q, k, v, *, tb=128, nq=2, window=3, band=256):
    b, s, w = q.shape
    tq = nq * tb
    assert s % tq == 0 and w % LANES == 0 and tq <= window * tb <= s
    assert band % LANES == 0 and tb // 2 <= LANES and band - tb // 2 <= tq and (window - 1) * tb <= tq
    n_pairs = w // LANES
    qspec = pl.BlockSpec((1, tq, w), lambda bi, i: (bi, i, 0))
    kvspec = pl.BlockSpec((1, s, w), lambda bi, i: (bi, 0, 0))
    return pl.pallas_call(
        functools.partial(_attn_body, tb=tb, nq=nq, window=window, band=band),
        grid=(b, s // tq),
        in_specs=[qspec, kvspec, kvspec],
        out_specs=qspec,
        out_shape=jax.ShapeDtypeStruct((b, s, w), BF16),
        scratch_shapes=[pltpu.VMEM((n_pairs, 2 * tb, tb), F32), pltpu.VMEM((n_pairs, 2 * tb, LANES), F32)],
        compiler_params=_params("parallel", "parallel"),
        name="stickbreak_attn",
    )(q, k, v)


def _merge_ffn_body(x_ref, sb_ref, cv_ref, gl_ref, gates_ref, wb_ref, wo_ref, mpost_ref,
                    pre_ref, post_ref, wg_ref, wu_ref, wd_ref, o_ref, acc_ref, *, fc):
    d = x_ref.shape[1]
    merged = gates_ref[:, 0:d].astype(F32) * _dot(sb_ref[...], wb_ref[0])
    merged = merged + gates_ref[:, d:2 * d].astype(F32) * _dot(cv_ref[...], wb_ref[1])
    merged = merged + gates_ref[:, 2 * d:3 * d].astype(F32) * _dot(gl_ref[...], wb_ref[2])
    m = _dot(merged.astype(BF16), wo_ref[...])
    x = x_ref[...] + _rms(m, mpost_ref[...])
    o_ref[...] = _ffn_tile(x, pre_ref[...], post_ref[...], wg_ref, wu_ref, wd_ref, acc_ref, fc)


def _merge_ffn(x, sb, cv, gl, gates, wb, wo, mpost, pre, post, wg, wu, wd, layer, *, tm=512, fc=512):
    t, d = x.shape
    w = sb.shape[1]

    def row(n):
        return pl.BlockSpec((tm, n), lambda i: (i, 0))

    vec = _resident((1, d))
    return pl.pallas_call(
        functools.partial(_merge_ffn_body, fc=fc),
        grid=(t // tm,),
        in_specs=[row(d), row(w), row(w), row(w), row(3 * d), _resident_layer(wb, layer),
                  _resident_layer(wo, layer), vec, vec, vec,
                  _resident_layer(wg, layer), _resident_layer(wu, layer), _resident_layer(wd, layer)],
        out_specs=row(d),
        out_shape=jax.ShapeDtypeStruct((t, d), F32),
        scratch_shapes=[pltpu.VMEM((tm, d), F32)],
        compiler_params=_params("parallel"),
        name="merge_ffn",
    )(x, sb, cv, gl, gates, wb, wo, mpost.reshape(1, d), pre.reshape(1, d), post.reshape(1, d),
      wg, wu, wd)


def kernel(x, norm_pre, norm_post, ffn1_w_gate, ffn1_w_up, ffn1_w_down, ffn2_w_gate, ffn2_w_up,
           ffn2_w_down, w_in, conv_w, conv_b, conv_ln_g, conv_ln_b, gla_w_alpha, gla_b_alpha,
           gla_norm_g, w_branch, w_out):
    b, s, d = x.shape
    t = b * s
    n_layers = norm_pre.shape[0]

    ffn1 = [a.astype(BF16) for a in (ffn1_w_gate, ffn1_w_up, ffn1_w_down)]
    ffn2 = [a.astype(BF16) for a in (ffn2_w_gate, ffn2_w_up, ffn2_w_down)]
    o_gate = w_in.shape[2] - N_BRANCHES * d
    o_lr = o_gate - GLA_GATE_RANK
    w_main = w_in[:, :, :o_lr].astype(BF16)
    w_lr = jnp.pad(w_in[:, :, o_lr:o_gate], ((0, 0), (0, 0), (0, LANES - GLA_GATE_RANK))).astype(BF16)
    w_gate = w_in[:, :, o_gate:].astype(BF16)
    walpha = jnp.pad(gla_w_alpha, ((0, 0), (0, LANES - GLA_GATE_RANK), (0, 0)))
    w_branch_b = w_branch.astype(BF16)
    w_out_b = w_out.astype(BF16)

    def flat(a):
        return a.reshape(t, a.shape[2])

    xf = x.reshape(t, d)
    for l in range(n_layers):
        xf = _ffn(xf, norm_pre[l, 0], norm_post[l, 0], *ffn1, l)
        q, k, v, gates, cv, gl = _mixer_in(
            xf.reshape(b, s, d), norm_pre[l, 1], w_main, w_lr, w_gate, l, walpha[l], gla_b_alpha[l],
            conv_w[l], conv_b[l], conv_ln_g[l], conv_ln_b[l], gla_norm_g[l])
        sb = _attn(q, k, v)
        xf = _merge_ffn(xf, flat(sb), flat(cv), flat(gl), flat(gates), w_branch_b, w_out_b,
                        norm_post[l, 1], norm_pre[l, 2], norm_post[l, 2], *ffn2, l)
    return xf.reshape(b, s, d)
```

```python
import functools

import jax
import jax.numpy as jnp
from jax import lax
from jax.experimental import pallas as pl
from jax.experimental.pallas import tpu as pltpu

NORM_EPS = 1e-6
BRANCH_WIDTH = 512
SB_HEAD_DIM = 64
CONV_WIDTH = 31
GLA_KEY_DIM = 256
GLA_HEADS = 4
GLA_HEAD_K = 64
GLA_HEAD_V = 128
GLA_GATE_RANK = 16
GLA_GATE_TAU = 16.0
GLA_CHUNK = 64
N_BRANCHES = 3
FFN_ROWS = 128
GATE_CHUNK = 256

DEAD_STICK = 110.0

LANES = 128
SUBLANES = 8
BF16_ROWS = 16
VMEM_LIMIT_BYTES = 56 * 1024 * 1024

F32 = jnp.float32
BF16 = jnp.bfloat16


def _dot(a, b):
    return jnp.dot(a, b, preferred_element_type=F32)


def _dot_nt(a, b):
    return lax.dot_general(a, b, (((1,), (1,)), ((), ())), preferred_element_type=F32)


def _dot_tn(a, b):
    return lax.dot_general(a, b, (((0,), (0,)), ((), ())), preferred_element_type=F32)


def _rms(x, g):
    return x * lax.rsqrt(jnp.mean(x * x, axis=-1, keepdims=True) + NORM_EPS) * g


def _sigmoid(x):
    return 1.0 / (1.0 + jnp.exp(-x))


def _params(*sem):
    return pltpu.CompilerParams(dimension_semantics=sem, vmem_limit_bytes=VMEM_LIMIT_BYTES)


def _resident(shape):
    nd = len(shape)
    return pl.BlockSpec(shape, lambda *_: (0,) * nd, pipeline_mode=pl.Buffered(1))


def _cast_jobs(srcs, layer, step_of, n_steps):
    in_specs, out_specs, out_shapes, n_blocks = [], [], [], []
    for src in srcs:
        _, r, c = src.shape
        rows = next(m for m in range(BF16_ROWS, r + 1, BF16_ROWS) if r % m == 0 and r // m <= n_steps)
        nb = r // rows
        in_specs.append(pl.BlockSpec(
            (None, rows, c), lambda *g, nb=nb: (layer, jnp.minimum(step_of(*g), nb - 1), 0)))
        out_specs.append(pl.BlockSpec(
            (rows, c), lambda *g, nb=nb: (jnp.minimum(step_of(*g), nb - 1), 0)))
        out_shapes.append(jax.ShapeDtypeStruct((r, c), BF16))
        n_blocks.append(nb)
    return in_specs, out_specs, out_shapes, n_blocks


def _run_casts(step, src_refs, dst_refs, n_blocks):
    for src, dst, nb in zip(src_refs, dst_refs, n_blocks):
        @pl.when(step < nb)
        def _(src=src, dst=dst):
            dst[...] = src[...].astype(BF16)


def _ffn_tile(x, pre, post, wg_ref, wu_ref, wd_ref, acc_ref, fc):
    tm = x.shape[0]
    d_ff = wg_ref.shape[1]
    slabs = [slice(r, r + FFN_ROWS) for r in range(0, tm, FFN_ROWS)]
    chunks = [(c0, min(c0 + fc, d_ff)) for c0 in range(0, d_ff, fc)]

    def hidden(hh, c0, c1):
        g = _dot(hh, wg_ref[:, c0:c1])
        return (g * _sigmoid(g) * _dot(hh, wu_ref[:, c0:c1])).astype(BF16)

    hs = [_rms(x[r], pre).astype(BF16) for r in slabs]
    h = jnp.concatenate(hs, axis=0)
    out = None
    for n, (c0, c1) in enumerate(chunks):
        if n == 0:
            a = jnp.concatenate([hidden(hh, c0, c1) for hh in hs], axis=0)
        else:
            a = hidden(h, c0, c1)
        if n == len(chunks) - 1:
            out = jnp.concatenate(
                [x[r] + 0.5 * _rms(acc_ref[r, :] + _dot(a[r], wd_ref[c0:c1, :]), post) for r in slabs], axis=0)
        elif n == 0:
            acc_ref[...] = _dot(a, wd_ref[c0:c1, :])
        else:
            acc_ref[...] += _dot(a, wd_ref[c0:c1, :])
    return out


def _ffn_body(x_ref, pre_ref, post_ref, wg_ref, wu_ref, wd_ref, win_ref,
              o_ref, wmain_ref, wlr_ref, wgate_ref, acc_ref, *, fc):
    o_ref[...] = _ffn_tile(x_ref[...], pre_ref[...], post_ref[...], wg_ref, wu_ref, wd_ref, acc_ref, fc)
    n_main = wmain_ref.shape[1]
    o_gate = n_main + GLA_GATE_RANK
    wmain_ref[...] = win_ref[:, :n_main].astype(BF16)
    lane = lax.broadcasted_iota(jnp.int32, (1, LANES), 1)
    wlr_ref[...] = jnp.where(lane < GLA_GATE_RANK, win_ref[:, n_main:n_main + LANES], 0.0).astype(BF16)
    wgate_ref[...] = win_ref[:, o_gate:].astype(BF16)


def _ffn(x, pre, post, wg, wu, wd, w_in, layer, *, tm=512, fc=512):
    t, d = x.shape
    n_steps = t // tm
    n_in = w_in.shape[2]
    n_gate = N_BRANCHES * d
    n_main = n_in - n_gate - GLA_GATE_RANK
    assert n_main % LANES == 0 and d % n_steps == 0 and (d // n_steps) % BF16_ROWS == 0
    rows = d // n_steps
    row = pl.BlockSpec((tm, d), lambda i: (i, 0))

    def wrow(n):
        return pl.BlockSpec((rows, n), lambda i: (i, 0))

    return pl.pallas_call(
        functools.partial(_ffn_body, fc=fc),
        grid=(n_steps,),
        in_specs=[row, _resident((1, d)), _resident((1, d)),
                  _resident(wg.shape), _resident(wu.shape), _resident(wd.shape),
                  pl.BlockSpec((None, rows, n_in), lambda i: (layer, i, 0))],
        out_specs=[row, wrow(n_main), wrow(LANES), wrow(n_gate)],
        out_shape=[jax.ShapeDtypeStruct((t, d), F32), jax.ShapeDtypeStruct((d, n_main), BF16),
                   jax.ShapeDtypeStruct((d, LANES), BF16), jax.ShapeDtypeStruct((d, n_gate), BF16)],
        scratch_shapes=[pltpu.VMEM((tm, d), F32)],
        compiler_params=_params("arbitrary"),
        name="ffn",
    )(x, pre.reshape(1, d), post.reshape(1, d), wg, wu, wd, w_in)


def _conv_tile(ext_ref, shift, w, bias, gain, beta, o_ref, *, tm, rb, halo):
    ne = rb + halo
    first = halo - (CONV_WIDTH - 1)

    def block(r0):
        shifted = _dot(shift, ext_ref[r0:r0 + ne, :])
        acc = bias
        for j in range(CONV_WIDTH):
            b = (first + j) % SUBLANES
            start = b * ne + (first + j - b)
            acc = acc + w[j:j + 1, :] * shifted[start:start + rb]
        mu = jnp.mean(acc, axis=-1, keepdims=True)
        yc = acc - mu
        var = jnp.mean(yc * yc, axis=-1, keepdims=True)
        yn = yc * lax.rsqrt(var + NORM_EPS) * gain + beta
        o_ref[0, r0:r0 + rb, :] = (yn * _sigmoid(yn)).astype(BF16)

    return [functools.partial(block, r0) for r0 in range(0, tm, rb)]


def _gla_tile(q, k, v, r, la, gain, state_ref, o_ref, *, tm):
    ch = GLA_CHUNK
    rr = lax.broadcasted_iota(jnp.int32, (ch, ch), 0)
    cc = lax.broadcasted_iota(jnp.int32, (ch, ch), 1)
    later = jnp.where(cc > rr, 1.0, 0.0).astype(BF16)
    key_head = lax.broadcasted_iota(jnp.int32, (1, GLA_KEY_DIM), 1) // GLA_HEAD_K

    n_chunks = tm // ch
    carry = [state_ref[...]]

    def chunk(n):
        state = carry[0]
        rows = slice(n * ch, (n + 1) * ch)
        la_c = la[rows]
        la_hi = la_c.astype(BF16)
        la_lo = (la_c - la_hi.astype(F32)).astype(BF16)
        decay_to_end = _dot(later, la_hi) + _dot(later, la_lo)
        chunk_decay = jnp.exp(decay_to_end[0:1, :] + la_c[0:1, :])
        k_dec = (k[rows].astype(F32) * jnp.exp(decay_to_end)).astype(BF16)
        state = state * chunk_decay + _dot_tn(v[rows], k_dec)
        st = state.astype(BF16)
        q_c = q[rows]
        normed = []
        for h in range(GLA_HEADS):
            qh = jnp.where(key_head == h, q_c, jnp.zeros_like(q_c))
            oh = _dot_nt(qh, st[h * GLA_HEAD_V:(h + 1) * GLA_HEAD_V, :])
            normed.append(oh * lax.rsqrt(jnp.mean(oh * oh, axis=-1, keepdims=True) + NORM_EPS))
        rg = r[rows].astype(F32)
        o_ref[0, rows, :] = (jnp.concatenate(normed, axis=1) * gain * (rg * _sigmoid(rg))).astype(BF16)
        carry[0] = state
        if n == n_chunks - 1:
            state_ref[...] = state

    return [functools.partial(chunk, n) for n in range(n_chunks)]


def _mixer_in_body(*refs, tm, rb, halo, n_blocks):
    n_cast = len(n_blocks)
    (x_ref, pre_ref, win_ref, wlr_ref, wgate_ref, walpha_ref, balpha_ref,
     shift_ref, cw_ref, cb_ref, cg_ref, cbeta_ref, gg_ref) = refs[:13]
    cast_src = refs[13:13 + n_cast]
    q_ref, k_ref, v_ref, gates_ref, cv_ref, gl_ref = refs[13 + n_cast:19 + n_cast]
    cast_dst = refs[19 + n_cast:19 + 2 * n_cast]
    ext_ref, state_ref = refs[19 + 2 * n_cast:]
    _run_casts(pl.program_id(0) * pl.num_programs(1) + pl.program_id(1), cast_src, cast_dst, n_blocks)
    w = BRANCH_WIDTH
    dk = GLA_KEY_DIM

    @pl.when(pl.program_id(1) == 0)
    def _():
        ext_ref[0:halo, :] = jnp.zeros((halo, w), BF16)
        state_ref[...] = jnp.zeros_like(state_ref)

    @pl.when(pl.program_id(1) > 0)
    def _():
        ext_ref[0:halo, :] = ext_ref[tm:tm + halo, :]

    h = _rms(x_ref[0], pre_ref[...]).astype(BF16)
    o_conv = 3 * w
    o_gla = o_conv + 2 * w
    o_lr = o_gla + 2 * dk + 2 * w

    qkv = _dot(h, win_ref[:, :o_conv])
    q_ref[0] = (qkv[:, :w] * (SB_HEAD_DIM ** -0.5)).astype(BF16)
    k_ref[0] = qkv[:, w:2 * w].astype(BF16)
    v_ref[0] = qkv[:, 2 * w:].astype(BF16)

    c = _dot(h, win_ref[:, o_conv:o_gla])
    ext_ref[halo:halo + tm, :] = (c[:, :w] * _sigmoid(c[:, w:])).astype(BF16)
    tasks = _conv_tile(ext_ref, shift_ref[...], cw_ref[...], cb_ref[...], cg_ref[...], cbeta_ref[...], cv_ref,
                       tm=tm, rb=rb, halo=halo)

    g4 = _dot(h, win_ref[:, o_gla:o_lr])
    lr = _dot(h, wlr_ref[...])
    a = jnp.dot(lr, walpha_ref[...], preferred_element_type=F32,
                precision=lax.Precision.HIGHEST) + balpha_ref[...]
    la = (jnp.minimum(a, 0.0) - jnp.log(1.0 + jnp.exp(-jnp.abs(a)))) * (1.0 / GLA_GATE_TAU)
    tasks += _gla_tile((g4[:, :dk] * (GLA_HEAD_K ** -0.5)).astype(BF16), g4[:, dk:2 * dk].astype(BF16),
                       g4[:, 2 * dk:2 * dk + w].astype(BF16), g4[:, 2 * dk + w:].astype(BF16), la,
                       gg_ref[...], state_ref, gl_ref, tm=tm)

    n_gate = gates_ref.shape[2]
    gate_cols = [(c0, min(c0 + GATE_CHUNK, n_gate)) for c0 in range(0, n_gate, GATE_CHUNK)]
    per = -(-len(tasks) // len(gate_cols))
    for j, (c0, c1) in enumerate(gate_cols):
        gates_ref[0, :, c0:c1] = _sigmoid(_dot(h, wgate_ref[:, c0:c1])).astype(BF16)
        for task in tasks[j * per:(j + 1) * per]:
            task()


def _mixer_in(x, pre, w_main, w_lr, w_gate, walpha, balpha, conv_w, conv_b, conv_g,
              conv_beta, gla_g, cast_srcs, layer, *, tm=512, rb=128, halo=32):
    b, s, d = x.shape
    w = BRANCH_WIDTH
    dk = GLA_KEY_DIM
    n_gate = w_gate.shape[1]
    assert w_main.shape[1] == 5 * w + 2 * dk + 2 * w and w_lr.shape[1] == LANES
    n_seq = s // tm
    c_in, c_out, c_shapes, n_blocks = _cast_jobs(cast_srcs, layer, lambda bi, i: bi * n_seq + i, b * n_seq)
    assert halo >= CONV_WIDTH - 1 and halo % (2 * SUBLANES) == 0 and tm % rb == 0 and s % tm == 0
    ne = rb + halo
    m = jnp.arange(ne)
    shift = jnp.concatenate([(m[None, :] == m[:, None] + j) for j in range(SUBLANES)], axis=0).astype(BF16)

    def row(n):
        return pl.BlockSpec((1, tm, n), lambda bi, i: (bi, i, 0))

    def out(n):
        return jax.ShapeDtypeStruct((b, s, n), BF16)

    vec = _resident((1, w))
    return pl.pallas_call(
        functools.partial(_mixer_in_body, tm=tm, rb=rb, halo=halo, n_blocks=tuple(n_blocks)),
        grid=(b, n_seq),
        in_specs=[row(d), _resident((1, d)), _resident(w_main.shape),
                  _resident(w_lr.shape), _resident(w_gate.shape),
                  _resident(walpha.shape), _resident((1, dk)), _resident(shift.shape),
                  _resident(conv_w.shape), vec, vec, vec, vec] + c_in,
        out_specs=[row(w), row(w), row(w), row(n_gate), row(w), row(w)] + c_out,
        out_shape=[out(w), out(w), out(w), out(n_gate), out(w), out(w)] + c_shapes,
        scratch_shapes=[pltpu.VMEM((halo + tm, w), BF16), pltpu.VMEM((w, dk), F32)],
        compiler_params=_params("arbitrary", "arbitrary"),
        name="mixer_in",
    )(x, pre.reshape(1, d), w_main, w_lr, w_gate, walpha, balpha.reshape(1, dk), shift,
      conv_w, conv_b.reshape(1, w), conv_g.reshape(1, w), conv_beta.reshape(1, w), gla_g.reshape(1, w),
      *cast_srcs)


def _attn_body(q_ref, k_ref, v_ref, o_ref, stick_ref, acc_ref, *, tb, nq, window, band):
    i = pl.program_id(1)
    n_pairs = q_ref.shape[2] // LANES
    hb = tb // 2
    wk = window * tb
    lane = lax.broadcasted_iota(jnp.int32, (1, LANES), 1)
    head0 = lane < SB_HEAD_DIM

    def pair(ref, r0, n, p):
        return ref[0, pl.ds(r0, n), p * LANES:(p + 1) * LANES]

    def stacked_q(p, r0, n):
        q = pair(q_ref, r0, n, p)
        zero = jnp.zeros_like(q)
        return jnp.concatenate([jnp.where(head0, q, zero), jnp.where(head0, zero, q)], axis=0)

    def log_terms(z):
        lg = jnp.log(1.0 + jnp.exp(-jnp.abs(z)))
        return jnp.maximum(z, 0.0) + lg, jnp.minimum(z, 0.0) - lg

    def write_out(acc, r0, n, p):
        o_ref[0, pl.ds(r0, n), p * LANES:(p + 1) * LANES] = jnp.where(head0, acc[:n], acc[n:]).astype(BF16)

    def tri(n):
        return jnp.where(lax.broadcasted_iota(jnp.int32, (n, n), 0) > lax.broadcasted_iota(jnp.int32, (n, n), 1),
                         1.0, 0.0)

    def query_row(n_rows, n_cols):
        r = lax.broadcasted_iota(jnp.int32, (2 * n_rows, n_cols), 0)
        return jnp.where(r >= n_rows, r - n_rows, r)

    def masked_pass(qs_list, w0_list, width, mask):
        sps, lss = [], []
        for (p, qs), w0 in zip(qs_list, w0_list):
            sp, ls = log_terms(_dot_nt(qs, pair(k_ref, w0, width, p)))
            sps.append(mask(sp))
            lss.append(ls)
        later = _dot(jnp.concatenate(sps, axis=0).astype(BF16), tri(width).astype(BF16))
        n = later.shape[0] // len(sps)
        out = []
        for e, ((p, _), w0) in enumerate(zip(qs_list, w0_list)):
            lt = later[e * n:(e + 1) * n]
            wgt = mask(jnp.exp(lss[e] - lt))
            out.append((_dot(wgt.astype(BF16), pair(v_ref, w0, width, p)), lt[:, 0:1] + sps[e][:, 0:1]))
        return out

    def last_tile_mask(keep, width):
        def mask(a):
            return jnp.concatenate([a[:, :width - LANES], jnp.where(keep, a[:, width - LANES:], 0.0)], axis=1)
        return mask

    def band_pass():
        keep = lax.broadcasted_iota(jnp.int32, (2 * hb, LANES), 1) < query_row(hb, LANES) + (LANES - hb)
        entries, starts, rows0 = [], [], []
        for half in range(2 * nq):
            r0 = half * hb
            w0 = pl.multiple_of(i * (nq * tb) + (r0 + hb - band), hb)
            for p in range(n_pairs):
                entries.append((p, stacked_q(p, r0, hb)))
                starts.append(w0)
                rows0.append(r0)
        res = masked_pass(entries, starts, band, last_tile_mask(keep, band))
        mins = [None] * nq
        for (acc, stick), (p, _), r0 in zip(res, entries, rows0):
            write_out(acc, r0, hb, p)
            blk = r0 // tb
            mins[blk] = stick if mins[blk] is None else jnp.minimum(mins[blk], stick)
        return [jnp.min(m) for m in mins]

    def window_pass(blk_r0, w0, keep, aligned):
        if aligned:
            mask = last_tile_mask(keep, wk)
        else:
            def mask(a):
                return jnp.where(keep, a, 0.0)
        entries = [(p, stacked_q(p, blk_r0, tb)) for p in range(n_pairs)]
        res = masked_pass(entries, [w0] * n_pairs, wk, mask)
        for (acc, _), p in zip(res, range(n_pairs)):
            write_out(acc, blk_r0, tb, p)
        return res

    def block_step(j0, p, blk_r0, stick, acc, suffix):
        sp, ls = log_terms(_dot_nt(stacked_q(p, blk_r0, tb), pair(k_ref, j0, tb, p)))
        sums = _dot(sp.astype(BF16), suffix)
        wgt = jnp.exp(ls - sums[:, :tb] - stick)
        return stick + sums[:, tb:], acc + _dot(wgt.astype(BF16), pair(v_ref, j0, tb, p))

    def min_stick(sticks):
        m = sticks[0]
        for s in sticks[1:]:
            m = jnp.minimum(m, s)
        return jnp.min(m)

    def exact_block(blk):
        blk_r0 = blk * tb
        ib = i * nq + blk
        keep = lax.broadcasted_iota(jnp.int32, (2 * tb, LANES), 1) < query_row(tb, LANES) + (LANES - tb)
        res = window_pass(blk_r0, pl.multiple_of((ib - (window - 1)) * tb, tb), keep, True)
        for p in range(n_pairs):
            acc_ref[p] = res[p][0]
            stick_ref[p] = jnp.broadcast_to(res[p][1], (2 * tb, tb))
        suffix = jnp.concatenate([tri(tb), jnp.ones((tb, tb), F32)], axis=1).astype(BF16)

        def cond(st):
            return jnp.logical_and(st[0] >= 0, st[1] > 0)

        def body(st):
            j0 = pl.multiple_of(st[0] * tb, tb)
            new = []
            for p in range(n_pairs):
                stick, acc = block_step(j0, p, blk_r0, stick_ref[p], acc_ref[p], suffix)
                stick_ref[p] = stick
                acc_ref[p] = acc
                new.append(stick)
            return st[0] - 1, (min_stick(new) < DEAD_STICK).astype(jnp.int32)

        alive = (min_stick([r[1] for r in res]) < DEAD_STICK).astype(jnp.int32)
        lax.while_loop(cond, body, (ib - window, alive))
        for p in range(n_pairs):
            write_out(acc_ref[p], blk_r0, tb, p)

    @pl.when(i > 0)
    def _():
        for blk, lowest in enumerate(band_pass()):
            pl.when(lowest < DEAD_STICK)(functools.partial(exact_block, blk))

    @pl.when(i == 0)
    def _():
        for blk in range(nq):
            keep = lax.broadcasted_iota(jnp.int32, (2 * tb, wk), 1) < query_row(tb, wk) + blk * tb
            window_pass(blk * tb, 0, keep, False)


def _attn(q, k, v, *, tb=128, nq=2, window=3, band=256):
    b, s, w = q.shape
    tq = nq * tb
    assert s % tq == 0 and w % LANES == 0 and tq <= window * tb <= s
    assert band % LANES == 0 and tb // 2 <= LANES and band - tb // 2 <= tq and (window - 1) * tb <= tq
    n_pairs = w // LANES
    qspec = pl.BlockSpec((1, tq, w), lambda bi, i: (bi, i, 0))
    kvspec = pl.BlockSpec((1, s, w), lambda bi, i: (bi, 0, 0))
    return pl.pallas_call(
        functools.partial(_attn_body, tb=tb, nq=nq, window=window, band=band),
        grid=(b, s // tq),
        in_specs=[qspec, kvspec, kvspec],
        out_specs=qspec,
        out_shape=jax.ShapeDtypeStruct((b, s, w), BF16),
        scratch_shapes=[pltpu.VMEM((n_pairs, 2 * tb, tb), F32), pltpu.VMEM((n_pairs, 2 * tb, LANES), F32)],
        compiler_params=_params("parallel", "parallel"),
        name="stickbreak_attn",
    )(q, k, v)


def _merge_ffn_body(*refs, fc, n_blocks):
    n_cast = len(n_blocks)
    (x_ref, sb_ref, cv_ref, gl_ref, gates_ref, wb_ref, wo_ref, mpost_ref,
     pre_ref, post_ref, wg_ref, wu_ref, wd_ref) = refs[:13]
    cast_src = refs[13:13 + n_cast]
    o_ref = refs[13 + n_cast]
    cast_dst = refs[14 + n_cast:14 + 2 * n_cast]
    acc_ref = refs[14 + 2 * n_cast]
    _run_casts(pl.program_id(0), cast_src, cast_dst, n_blocks)
    d = x_ref.shape[1]
    w = sb_ref.shape[1]
    merged = gates_ref[:, 0:d].astype(F32) * _dot(sb_ref[...], wb_ref[0:w, :])
    merged = merged + gates_ref[:, d:2 * d].astype(F32) * _dot(cv_ref[...], wb_ref[w:2 * w, :])
    merged = merged + gates_ref[:, 2 * d:3 * d].astype(F32) * _dot(gl_ref[...], wb_ref[2 * w:3 * w, :])
    m = _dot(merged.astype(BF16), wo_ref[...])
    x = x_ref[...] + _rms(m, mpost_ref[...])
    o_ref[...] = _ffn_tile(x, pre_ref[...], post_ref[...], wg_ref, wu_ref, wd_ref, acc_ref, fc)


def _merge_ffn(x, sb, cv, gl, gates, wb, wo, mpost, pre, post, wg, wu, wd, cast_srcs, layer,
               *, tm=512, fc=512):
    t, d = x.shape
    w = sb.shape[1]
    c_in, c_out, c_shapes, n_blocks = _cast_jobs(cast_srcs, layer, lambda i: i, t // tm)

    def row(n):
        return pl.BlockSpec((tm, n), lambda i: (i, 0))

    vec = _resident((1, d))
    return pl.pallas_call(
        functools.partial(_merge_ffn_body, fc=fc, n_blocks=tuple(n_blocks)),
        grid=(t // tm,),
        in_specs=[row(d), row(w), row(w), row(w), row(3 * d), _resident(wb.shape),
                  _resident(wo.shape), vec, vec, vec,
                  _resident(wg.shape), _resident(wu.shape), _resident(wd.shape)] + c_in,
        out_specs=[row(d)] + c_out,
        out_shape=[jax.ShapeDtypeStruct((t, d), F32)] + c_shapes,
        scratch_shapes=[pltpu.VMEM((tm, d), F32)],
        compiler_params=_params("arbitrary"),
        name="merge_ffn",
    )(x, sb, cv, gl, gates, wb, wo, mpost.reshape(1, d), pre.reshape(1, d), post.reshape(1, d),
      wg, wu, wd, *cast_srcs)


def kernel(x, norm_pre, norm_post, ffn1_w_gate, ffn1_w_up, ffn1_w_down, ffn2_w_gate, ffn2_w_up,
           ffn2_w_down, w_in, conv_w, conv_b, conv_ln_g, conv_ln_b, gla_w_alpha, gla_b_alpha,
           gla_norm_g, w_branch, w_out):
    b, s, d = x.shape
    t = b * s
    n_layers = norm_pre.shape[0]
    walpha = jnp.pad(gla_w_alpha, ((0, 0), (0, LANES - GLA_GATE_RANK), (0, 0)))
    w_branch2 = w_branch.reshape(n_layers, N_BRANCHES * BRANCH_WIDTH, d)
    ffn1_srcs = [ffn1_w_gate, ffn1_w_up, ffn1_w_down]
    mix_srcs = [ffn2_w_gate, ffn2_w_up, ffn2_w_down, w_branch2, w_out]

    def flat(a):
        return a.reshape(t, a.shape[2])

    ffn1 = [a[0].astype(BF16) for a in ffn1_srcs]
    xf = x.reshape(t, d)
    for l in range(n_layers):
        xf, w_main, w_lr, w_gate = _ffn(xf, norm_pre[l, 0], norm_post[l, 0], *ffn1, w_in, l)
        q, k, v, gates, cv, gl, *mix_w = _mixer_in(
            xf.reshape(b, s, d), norm_pre[l, 1], w_main, w_lr, w_gate, walpha[l], gla_b_alpha[l],
            conv_w[l], conv_b[l], conv_ln_g[l], conv_ln_b[l], gla_norm_g[l], mix_srcs, l)
        sb = _attn(q, k, v)
        ffn2, (wb, wo) = mix_w[:3], mix_w[3:]
        nxt = ffn1_srcs if l + 1 < n_layers else []
        xf, *ffn1 = _merge_ffn(xf, flat(sb), flat(cv), flat(gl), flat(gates), wb, wo,
                               norm_post[l, 1], norm_pre[l, 2], norm_post[l, 2], *ffn2, nxt, l + 1)
    return xf.reshape(b, s, d)
```

```python
import functools

import jax
import jax.numpy as jnp
from jax import lax
from jax.experimental import pallas as pl
from jax.experimental.pallas import tpu as pltpu

NORM_EPS = 1e-6
BRANCH_WIDTH = 512
SB_HEAD_DIM = 64
CONV_WIDTH = 31
GLA_KEY_DIM = 256
GLA_HEADS = 4
GLA_HEAD_K = 64
GLA_HEAD_V = 128
GLA_GATE_RANK = 16
GLA_GATE_TAU = 16.0
GLA_CHUNK = 64
N_BRANCHES = 3
FFN_ROWS = 128
GATE_CHUNK = 256

DEAD_STICK = 110.0

LANES = 128
SUBLANES = 8
BF16_ROWS = 16
VMEM_LIMIT_BYTES = 56 * 1024 * 1024

F32 = jnp.float32
BF16 = jnp.bfloat16


def _dot(a, b):
    return jnp.dot(a, b, preferred_element_type=F32)


def _dot_nt(a, b):
    return lax.dot_general(a, b, (((1,), (1,)), ((), ())), preferred_element_type=F32)


def _dot_tn(a, b):
    return lax.dot_general(a, b, (((0,), (0,)), ((), ())), preferred_element_type=F32)


def _rms(x, g):
    return x * lax.rsqrt(jnp.mean(x * x, axis=-1, keepdims=True) + NORM_EPS) * g


def _sigmoid(x):
    return 1.0 / (1.0 + jnp.exp(-x))


def _params(*sem):
    return pltpu.CompilerParams(dimension_semantics=sem, vmem_limit_bytes=VMEM_LIMIT_BYTES)


def _resident(shape):
    nd = len(shape)
    return pl.BlockSpec(shape, lambda *_: (0,) * nd, pipeline_mode=pl.Buffered(1))


def _cast_jobs(srcs, layer, step_of, n_steps):
    in_specs, out_specs, out_shapes, n_blocks = [], [], [], []
    for src in srcs:
        _, r, c = src.shape
        rows = next(m for m in range(BF16_ROWS, r + 1, BF16_ROWS) if r % m == 0 and r // m <= n_steps)
        nb = r // rows
        in_specs.append(pl.BlockSpec(
            (None, rows, c), lambda *g, nb=nb: (layer, jnp.minimum(step_of(*g), nb - 1), 0)))
        out_specs.append(pl.BlockSpec(
            (rows, c), lambda *g, nb=nb: (jnp.minimum(step_of(*g), nb - 1), 0)))
        out_shapes.append(jax.ShapeDtypeStruct((r, c), BF16))
        n_blocks.append(nb)
    return in_specs, out_specs, out_shapes, n_blocks


def _run_casts(step, src_refs, dst_refs, n_blocks):
    for src, dst, nb in zip(src_refs, dst_refs, n_blocks):
        @pl.when(step < nb)
        def _(src=src, dst=dst):
            dst[...] = src[...].astype(BF16)


def _ffn_tile(x, pre, post, wg_ref, wu_ref, wd_ref, acc_ref, fc):
    tm = x.shape[0]
    d_ff = wg_ref.shape[1]
    slabs = [slice(r, r + FFN_ROWS) for r in range(0, tm, FFN_ROWS)]
    chunks = [(c0, min(c0 + fc, d_ff)) for c0 in range(0, d_ff, fc)]

    def hidden(hh, c0, c1):
        g = _dot(hh, wg_ref[:, c0:c1])
        return (g * _sigmoid(g) * _dot(hh, wu_ref[:, c0:c1])).astype(BF16)

    hs = [_rms(x[r], pre).astype(BF16) for r in slabs]
    h = jnp.concatenate(hs, axis=0)
    out = None
    for n, (c0, c1) in enumerate(chunks):
        if n == 0:
            a = jnp.concatenate([hidden(hh, c0, c1) for hh in hs], axis=0)
        else:
            a = hidden(h, c0, c1)
        if n == len(chunks) - 1:
            out = jnp.concatenate(
                [x[r] + 0.5 * _rms(acc_ref[r, :] + _dot(a[r], wd_ref[c0:c1, :]), post) for r in slabs], axis=0)
        elif n == 0:
            acc_ref[...] = _dot(a, wd_ref[c0:c1, :])
        else:
            acc_ref[...] += _dot(a, wd_ref[c0:c1, :])
    return out


def _ffn_body(x_ref, pre_ref, post_ref, wg_ref, wu_ref, wd_ref, wint_ref, o_ref, wt_ref, acc_ref, *, fc):
    o_ref[...] = _ffn_tile(x_ref[...], pre_ref[...], post_ref[...], wg_ref, wu_ref, wd_ref, acc_ref, fc)
    wt_ref[...] = wint_ref[...].astype(BF16)


def _ffn(x, pre, post, wg, wu, wd, w_in_t, layer, *, tm=512, fc=512):
    t, d = x.shape
    n_steps = t // tm
    n_in = w_in_t.shape[1]
    rows = -(-n_in // (n_steps * BF16_ROWS)) * BF16_ROWS
    last = -(-n_in // rows) - 1
    row = pl.BlockSpec((tm, d), lambda i: (i, 0))
    return pl.pallas_call(
        functools.partial(_ffn_body, fc=fc),
        grid=(n_steps,),
        in_specs=[row, _resident((1, d)), _resident((1, d)),
                  _resident(wg.shape), _resident(wu.shape), _resident(wd.shape),
                  pl.BlockSpec((None, rows, d), lambda i: (layer, jnp.minimum(i, last), 0))],
        out_specs=[row, pl.BlockSpec((rows, d), lambda i: (jnp.minimum(i, last), 0))],
        out_shape=[jax.ShapeDtypeStruct((t, d), F32), jax.ShapeDtypeStruct((n_in, d), BF16)],
        scratch_shapes=[pltpu.VMEM((tm, d), F32)],
        compiler_params=_params("arbitrary"),
        name="ffn",
    )(x, pre.reshape(1, d), post.reshape(1, d), wg, wu, wd, w_in_t)


def _conv_tile(ext_ref, shift, w, bias, gain, beta, o_ref, *, tm, rb, halo):
    ne = rb + halo
    first = halo - (CONV_WIDTH - 1)

    def block(r0):
        shifted = _dot(shift, ext_ref[r0:r0 + ne, :])
        acc = bias
        for j in range(CONV_WIDTH):
            b = (first + j) % SUBLANES
            start = b * ne + (first + j - b)
            acc = acc + w[j:j + 1, :] * shifted[start:start + rb]
        mu = jnp.mean(acc, axis=-1, keepdims=True)
        yc = acc - mu
        var = jnp.mean(yc * yc, axis=-1, keepdims=True)
        yn = yc * lax.rsqrt(var + NORM_EPS) * gain + beta
        o_ref[0, r0:r0 + rb, :] = (yn * _sigmoid(yn)).astype(BF16)

    return [functools.partial(block, r0) for r0 in range(0, tm, rb)]


def _gla_tile(q, k, v, r, la, gain, state_ref, o_ref, *, tm):
    ch = GLA_CHUNK
    rr = lax.broadcasted_iota(jnp.int32, (ch, ch), 0)
    cc = lax.broadcasted_iota(jnp.int32, (ch, ch), 1)
    later = jnp.where(cc > rr, 1.0, 0.0).astype(BF16)
    key_head = lax.broadcasted_iota(jnp.int32, (1, GLA_KEY_DIM), 1) // GLA_HEAD_K

    n_chunks = tm // ch
    carry = [state_ref[...]]

    def chunk(n):
        state = carry[0]
        rows = slice(n * ch, (n + 1) * ch)
        la_c = la[rows]
        la_hi = la_c.astype(BF16)
        la_lo = (la_c - la_hi.astype(F32)).astype(BF16)
        decay_to_end = _dot(later, la_hi) + _dot(later, la_lo)
        chunk_decay = jnp.exp(decay_to_end[0:1, :] + la_c[0:1, :])
        k_dec = (k[rows].astype(F32) * jnp.exp(decay_to_end)).astype(BF16)
        state = state * chunk_decay + _dot_tn(v[rows], k_dec)
        st = state.astype(BF16)
        q_c = q[rows]
        normed = []
        for h in range(GLA_HEADS):
            qh = jnp.where(key_head == h, q_c, jnp.zeros_like(q_c))
            oh = _dot_nt(qh, st[h * GLA_HEAD_V:(h + 1) * GLA_HEAD_V, :])
            normed.append(oh * lax.rsqrt(jnp.mean(oh * oh, axis=-1, keepdims=True) + NORM_EPS))
        rg = r[rows].astype(F32)
        o_ref[0, rows, :] = (jnp.concatenate(normed, axis=1) * gain * (rg * _sigmoid(rg))).astype(BF16)
        carry[0] = state
        if n == n_chunks - 1:
            state_ref[...] = state

    return [functools.partial(chunk, n) for n in range(n_chunks)]


def _mixer_in_body(*refs, tm, rb, halo, n_blocks):
    n_cast = len(n_blocks)
    (x_ref, pre_ref, wt_ref, walpha_ref, balpha_ref,
     shift_ref, cw_ref, cb_ref, cg_ref, cbeta_ref, gg_ref) = refs[:11]
    cast_src = refs[11:11 + n_cast]
    q_ref, k_ref, v_ref, gates_ref, cv_ref, gl_ref = refs[11 + n_cast:17 + n_cast]
    cast_dst = refs[17 + n_cast:17 + 2 * n_cast]
    ext_ref, state_ref = refs[17 + 2 * n_cast:]
    _run_casts(pl.program_id(0) * pl.num_programs(1) + pl.program_id(1), cast_src, cast_dst, n_blocks)
    w = BRANCH_WIDTH
    dk = GLA_KEY_DIM

    @pl.when(pl.program_id(1) == 0)
    def _():
        ext_ref[0:halo, :] = jnp.zeros((halo, w), BF16)
        state_ref[...] = jnp.zeros_like(state_ref)

    @pl.when(pl.program_id(1) > 0)
    def _():
        ext_ref[0:halo, :] = ext_ref[tm:tm + halo, :]

    h = _rms(x_ref[0], pre_ref[...]).astype(BF16)
    o_conv = 3 * w
    o_gla = o_conv + 2 * w
    o_lr = o_gla + 2 * dk + 2 * w
    o_gate = o_lr + GLA_GATE_RANK

    qkv = _dot_nt(h, wt_ref[:o_conv, :])
    q_ref[0] = (qkv[:, :w] * (SB_HEAD_DIM ** -0.5)).astype(BF16)
    k_ref[0] = qkv[:, w:2 * w].astype(BF16)
    v_ref[0] = qkv[:, 2 * w:].astype(BF16)

    c = _dot_nt(h, wt_ref[o_conv:o_gla, :])
    ext_ref[halo:halo + tm, :] = (c[:, :w] * _sigmoid(c[:, w:])).astype(BF16)
    tasks = _conv_tile(ext_ref, shift_ref[...], cw_ref[...], cb_ref[...], cg_ref[...], cbeta_ref[...], cv_ref,
                       tm=tm, rb=rb, halo=halo)

    g4 = _dot_nt(h, wt_ref[o_gla:o_lr, :])
    lane = lax.broadcasted_iota(jnp.int32, (1, LANES), 1)
    lr = jnp.where(lane < GLA_GATE_RANK, _dot_nt(h, wt_ref[o_lr:o_lr + LANES, :]), 0.0)
    a = jnp.dot(lr, walpha_ref[...], preferred_element_type=F32,
                precision=lax.Precision.HIGHEST) + balpha_ref[...]
    la = (jnp.minimum(a, 0.0) - jnp.log(1.0 + jnp.exp(-jnp.abs(a)))) * (1.0 / GLA_GATE_TAU)
    tasks += _gla_tile((g4[:, :dk] * (GLA_HEAD_K ** -0.5)).astype(BF16), g4[:, dk:2 * dk].astype(BF16),
                       g4[:, 2 * dk:2 * dk + w].astype(BF16), g4[:, 2 * dk + w:].astype(BF16), la,
                       gg_ref[...], state_ref, gl_ref, tm=tm)

    n_gate = gates_ref.shape[2]
    gate_cols = [(c0, min(c0 + GATE_CHUNK, n_gate)) for c0 in range(0, n_gate, GATE_CHUNK)]
    per = -(-len(tasks) // len(gate_cols))
    for j, (c0, c1) in enumerate(gate_cols):
        gates_ref[0, :, c0:c1] = _sigmoid(_dot_nt(h, wt_ref[o_gate + c0:o_gate + c1, :])).astype(BF16)
        for task in tasks[j * per:(j + 1) * per]:
            task()


def _mixer_in(x, pre, w_t, walpha, balpha, conv_w, conv_b, conv_g,
              conv_beta, gla_g, cast_srcs, layer, *, tm=512, rb=128, halo=32):
    b, s, d = x.shape
    w = BRANCH_WIDTH
    dk = GLA_KEY_DIM
    n_gate = N_BRANCHES * d
    assert w_t.shape[0] == 5 * w + 2 * dk + 2 * w + GLA_GATE_RANK + n_gate
    n_seq = s // tm
    c_in, c_out, c_shapes, n_blocks = _cast_jobs(cast_srcs, layer, lambda bi, i: bi * n_seq + i, b * n_seq)
    assert halo >= CONV_WIDTH - 1 and halo % (2 * SUBLANES) == 0 and tm % rb == 0 and s % tm == 0
    ne = rb + halo
    m = jnp.arange(ne)
    shift = jnp.concatenate([(m[None, :] == m[:, None] + j) for j in range(SUBLANES)], axis=0).astype(BF16)

    def row(n):
        return pl.BlockSpec((1, tm, n), lambda bi, i: (bi, i, 0))

    def out(n):
        return jax.ShapeDtypeStruct((b, s, n), BF16)

    vec = _resident((1, w))
    return pl.pallas_call(
        functools.partial(_mixer_in_body, tm=tm, rb=rb, halo=halo, n_blocks=tuple(n_blocks)),
        grid=(b, n_seq),
        in_specs=[row(d), _resident((1, d)), _resident(w_t.shape),
                  _resident(walpha.shape), _resident((1, dk)), _resident(shift.shape),
                  _resident(conv_w.shape), vec, vec, vec, vec] + c_in,
        out_specs=[row(w), row(w), row(w), row(n_gate), row(w), row(w)] + c_out,
        out_shape=[out(w), out(w), out(w), out(n_gate), out(w), out(w)] + c_shapes,
        scratch_shapes=[pltpu.VMEM((halo + tm, w), BF16), pltpu.VMEM((w, dk), F32)],
        compiler_params=_params("arbitrary", "arbitrary"),
        name="mixer_in",
    )(x, pre.reshape(1, d), w_t, walpha, balpha.reshape(1, dk), shift,
      conv_w, conv_b.reshape(1, w), conv_g.reshape(1, w), conv_beta.reshape(1, w), gla_g.reshape(1, w),
      *cast_srcs)


def _attn_body(q_ref, k_ref, v_ref, o_ref, stick_ref, acc_ref, *, tb, nq, window, band):
    i = pl.program_id(1)
    n_pairs = q_ref.shape[2] // LANES
    hb = tb // 2
    wk = window * tb
    lane = lax.broadcasted_iota(jnp.int32, (1, LANES), 1)
    head0 = lane < SB_HEAD_DIM

    def pair(ref, r0, n, p):
        return ref[0, pl.ds(r0, n), p * LANES:(p + 1) * LANES]

    def stacked_q(p, r0, n):
        q = pair(q_ref, r0, n, p)
        zero = jnp.zeros_like(q)
        return jnp.concatenate([jnp.where(head0, q, zero), jnp.where(head0, zero, q)], axis=0)

    def log_terms(z):
        lg = jnp.log(1.0 + jnp.exp(-jnp.abs(z)))
        return jnp.maximum(z, 0.0) + lg, jnp.minimum(z, 0.0) - lg

    def write_out(acc, r0, n, p):
        o_ref[0, pl.ds(r0, n), p * LANES:(p + 1) * LANES] = jnp.where(head0, acc[:n], acc[n:]).astype(BF16)

    def tri(n):
        return jnp.where(lax.broadcasted_iota(jnp.int32, (n, n), 0) > lax.broadcasted_iota(jnp.int32, (n, n), 1),
                         1.0, 0.0)

    def query_row(n_rows, n_cols):
        r = lax.broadcasted_iota(jnp.int32, (2 * n_rows, n_cols), 0)
        return jnp.where(r >= n_rows, r - n_rows, r)

    def masked_pass(qs_list, w0_list, width, mask):
        sps, lss = [], []
        for (p, qs), w0 in zip(qs_list, w0_list):
            sp, ls = log_terms(_dot_nt(qs, pair(k_ref, w0, width, p)))
            sps.append(mask(sp))
            lss.append(ls)
        later = _dot(jnp.concatenate(sps, axis=0).astype(BF16), tri(width).astype(BF16))
        n = later.shape[0] // len(sps)
        out = []
        for e, ((p, _), w0) in enumerate(zip(qs_list, w0_list)):
            lt = later[e * n:(e + 1) * n]
            wgt = mask(jnp.exp(lss[e] - lt))
            out.append((_dot(wgt.astype(BF16), pair(v_ref, w0, width, p)), lt[:, 0:1] + sps[e][:, 0:1]))
        return out

    def last_tile_mask(keep, width):
        def mask(a):
            return jnp.concatenate([a[:, :width - LANES], jnp.where(keep, a[:, width - LANES:], 0.0)], axis=1)
        return mask

    def band_pass():
        keep = lax.broadcasted_iota(jnp.int32, (2 * hb, LANES), 1) < query_row(hb, LANES) + (LANES - hb)
        entries, starts, rows0 = [], [], []
        for half in range(2 * nq):
            r0 = half * hb
            w0 = pl.multiple_of(i * (nq * tb) + (r0 + hb - band), hb)
            for p in range(n_pairs):
                entries.append((p, stacked_q(p, r0, hb)))
                starts.append(w0)
                rows0.append(r0)
        res = masked_pass(entries, starts, band, last_tile_mask(keep, band))
        mins = [None] * nq
        for (acc, stick), (p, _), r0 in zip(res, entries, rows0):
            write_out(acc, r0, hb, p)
            blk = r0 // tb
            mins[blk] = stick if mins[blk] is None else jnp.minimum(mins[blk], stick)
        return [jnp.min(m) for m in mins]

    def window_pass(blk_r0, w0, keep, aligned):
        if aligned:
            mask = last_tile_mask(keep, wk)
        else:
            def mask(a):
                return jnp.where(keep, a, 0.0)
        entries = [(p, stacked_q(p, blk_r0, tb)) for p in range(n_pairs)]
        res = masked_pass(entries, [w0] * n_pairs, wk, mask)
        for (acc, _), p in zip(res, range(n_pairs)):
            write_out(acc, blk_r0, tb, p)
        return res

    def block_step(j0, p, blk_r0, stick, acc, suffix):
        sp, ls = log_terms(_dot_nt(stacked_q(p, blk_r0, tb), pair(k_ref, j0, tb, p)))
        sums = _dot(sp.astype(BF16), suffix)
        wgt = jnp.exp(ls - sums[:, :tb] - stick)
        return stick + sums[:, tb:], acc + _dot(wgt.astype(BF16), pair(v_ref, j0, tb, p))

    def min_stick(sticks):
        m = sticks[0]
        for s in sticks[1:]:
            m = jnp.minimum(m, s)
        return jnp.min(m)

    def exact_block(blk):
        blk_r0 = blk * tb
        ib = i * nq + blk
        keep = lax.broadcasted_iota(jnp.int32, (2 * tb, LANES), 1) < query_row(tb, LANES) + (LANES - tb)
        res = window_pass(blk_r0, pl.multiple_of((ib - (window - 1)) * tb, tb), keep, True)
        for p in range(n_pairs):
            acc_ref[p] = res[p][0]
            stick_ref[p] = jnp.broadcast_to(res[p][1], (2 * tb, tb))
        suffix = jnp.concatenate([tri(tb), jnp.ones((tb, tb), F32)], axis=1).astype(BF16)

        def cond(st):
            return jnp.logical_and(st[0] >= 0, st[1] > 0)

        def body(st):
            j0 = pl.multiple_of(st[0] * tb, tb)
            new = []
            for p in range(n_pairs):
                stick, acc = block_step(j0, p, blk_r0, stick_ref[p], acc_ref[p], suffix)
                stick_ref[p] = stick
                acc_ref[p] = acc
                new.append(stick)
            return st[0] - 1, (min_stick(new) < DEAD_STICK).astype(jnp.int32)

        alive = (min_stick([r[1] for r in res]) < DEAD_STICK).astype(jnp.int32)
        lax.while_loop(cond, body, (ib - window, alive))
        for p in range(n_pairs):
            write_out(acc_ref[p], blk_r0, tb, p)

    @pl.when(i > 0)
    def _():
        for blk, lowest in enumerate(band_pass()):
            pl.when(lowest < DEAD_STICK)(functools.partial(exact_block, blk))

    @pl.when(i == 0)
    def _():
        for blk in range(nq):
            keep = lax.broadcasted_iota(jnp.int32, (2 * tb, wk), 1) < query_row(tb, wk) + blk * tb
            window_pass(blk * tb, 0, keep, False)


def _attn(q, k, v, *, tb=128, nq=2, window=3, band=256):
    b, s, w = q.shape
    tq = nq * tb
    assert s % tq == 0 and w % LANES == 0 and tq <= window * tb <= s
    assert band % LANES == 0 and tb // 2 <= LANES and band - tb // 2 <= tq and (window - 1) * tb <= tq
    n_pairs = w // LANES
    qspec = pl.BlockSpec((1, tq, w), lambda bi, i: (bi, i, 0))
    kvspec = pl.BlockSpec((1, s, w), lambda bi, i: (bi, 0, 0))
    return pl.pallas_call(
        functools.partial(_attn_body, tb=tb, nq=nq, window=window, band=band),
        grid=(b, s // tq),
        in_specs=[qspec, kvspec, kvspec],
        out_specs=qspec,
        out_shape=jax.ShapeDtypeStruct((b, s, w), BF16),
        scratch_shapes=[pltpu.VMEM((n_pairs, 2 * tb, tb), F32), pltpu.VMEM((n_pairs, 2 * tb, LANES), F32)],
        compiler_params=_params("parallel", "parallel"),
        name="stickbreak_attn",
    )(q, k, v)


def _merge_ffn_body(*refs, fc, n_blocks):
    n_cast = len(n_blocks)
    (x_ref, sb_ref, cv_ref, gl_ref, gates_ref, wb_ref, wo_ref, mpost_ref,
     pre_ref, post_ref, wg_ref, wu_ref, wd_ref) = refs[:13]
    cast_src = refs[13:13 + n_cast]
    o_ref = refs[13 + n_cast]
    cast_dst = refs[14 + n_cast:14 + 2 * n_cast]
    acc_ref = refs[14 + 2 * n_cast]
    _run_casts(pl.program_id(0), cast_src, cast_dst, n_blocks)
    d = x_ref.shape[1]
    w = sb_ref.shape[1]
    merged = gates_ref[:, 0:d].astype(F32) * _dot(sb_ref[...], wb_ref[0:w, :])
    merged = merged + gates_ref[:, d:2 * d].astype(F32) * _dot(cv_ref[...], wb_ref[w:2 * w, :])
    merged = merged + gates_ref[:, 2 * d:3 * d].astype(F32) * _dot(gl_ref[...], wb_ref[2 * w:3 * w, :])
    m = _dot(merged.astype(BF16), wo_ref[...])
    x = x_ref[...] + _rms(m, mpost_ref[...])
    o_ref[...] = _ffn_tile(x, pre_ref[...], post_ref[...], wg_ref, wu_ref, wd_ref, acc_ref, fc)


def _merge_ffn(x, sb, cv, gl, gates, wb, wo, mpost, pre, post, wg, wu, wd, cast_srcs, layer,
               *, tm=512, fc=512):
    t, d = x.shape
    w = sb.shape[1]
    c_in, c_out, c_shapes, n_blocks = _cast_jobs(cast_srcs, layer, lambda i: i, t // tm)

    def row(n):
        return pl.BlockSpec((tm, n), lambda i: (i, 0))

    vec = _resident((1, d))
    return pl.pallas_call(
        functools.partial(_merge_ffn_body, fc=fc, n_blocks=tuple(n_blocks)),
        grid=(t // tm,),
        in_specs=[row(d), row(w), row(w), row(w), row(3 * d), _resident(wb.shape),
                  _resident(wo.shape), vec, vec, vec,
                  _resident(wg.shape), _resident(wu.shape), _resident(wd.shape)] + c_in,
        out_specs=[row(d)] + c_out,
        out_shape=[jax.ShapeDtypeStruct((t, d), F32)] + c_shapes,
        scratch_shapes=[pltpu.VMEM((tm, d), F32)],
        compiler_params=_params("arbitrary"),
        name="merge_ffn",
    )(x, sb, cv, gl, gates, wb, wo, mpost.reshape(1, d), pre.reshape(1, d), post.reshape(1, d),
      wg, wu, wd, *cast_srcs)


def kernel(x, norm_pre, norm_post, ffn1_w_gate, ffn1_w_up, ffn1_w_down, ffn2_w_gate, ffn2_w_up,
           ffn2_w_down, w_in, conv_w, conv_b, conv_ln_g, conv_ln_b, gla_w_alpha, gla_b_alpha,
           gla_norm_g, w_branch, w_out):
    b, s, d = x.shape
    t = b * s
    n_layers = norm_pre.shape[0]
    walpha = jnp.pad(gla_w_alpha, ((0, 0), (0, LANES - GLA_GATE_RANK), (0, 0)))
    w_branch2 = w_branch.reshape(n_layers, N_BRANCHES * BRANCH_WIDTH, d)
    w_in_t = jnp.swapaxes(w_in, 1, 2)
    ffn1_srcs = [ffn1_w_gate, ffn1_w_up, ffn1_w_down]
    mix_srcs = [ffn2_w_gate, ffn2_w_up, ffn2_w_down, w_branch2, w_out]

    def flat(a):
        return a.reshape(t, a.shape[2])

    ffn1 = [a[0].astype(BF16) for a in ffn1_srcs]
    xf = x.reshape(t, d)
    for l in range(n_layers):
        xf, w_t = _ffn(xf, norm_pre[l, 0], norm_post[l, 0], *ffn1, w_in_t, l)
        q, k, v, gates, cv, gl, *mix_w = _mixer_in(
            xf.reshape(b, s, d), norm_pre[l, 1], w_t, walpha[l], gla_b_alpha[l],
            conv_w[l], conv_b[l], conv_ln_g[l], conv_ln_b[l], gla_norm_g[l], mix_srcs, l)
        sb = _attn(q, k, v)
        ffn2, (wb, wo) = mix_w[:3], mix_w[3:]
        nxt = ffn1_srcs if l + 1 < n_layers else []
        xf, *ffn1 = _merge_ffn(xf, flat(sb), flat(cv), flat(gl), flat(gates), wb, wo,
                               norm_post[l, 1], norm_pre[l, 2], norm_post[l, 2], *ffn2, nxt, l + 1)
    return xf.reshape(b, s, d)
```

```python
import functools

import jax
import jax.numpy as jnp
from jax import lax
from jax.experimental import pallas as pl
from jax.experimental.pallas import tpu as pltpu

NORM_EPS = 1e-6
BRANCH_WIDTH = 512
SB_HEAD_DIM = 64
CONV_WIDTH = 31
GLA_KEY_DIM = 256
GLA_HEADS = 4
GLA_HEAD_K = 64
GLA_HEAD_V = 128
GLA_GATE_RANK = 16
GLA_GATE_TAU = 16.0
GLA_CHUNK = 64
N_BRANCHES = 3
FFN_ROWS = 128
GATE_CHUNK = 256

DEAD_STICK = 105.0

LANES = 128
SUBLANES = 8
BF16_ROWS = 16
VMEM_LIMIT_BYTES = 56 * 1024 * 1024

F32 = jnp.float32
BF16 = jnp.bfloat16


def _dot(a, b):
    return jnp.dot(a, b, preferred_element_type=F32)


def _dot_nt(a, b):
    return lax.dot_general(a, b, (((1,), (1,)), ((), ())), preferred_element_type=F32)


def _dot_tn(a, b):
    return lax.dot_general(a, b, (((0,), (0,)), ((), ())), preferred_element_type=F32)


def _rms(x, g):
    return x * lax.rsqrt(jnp.mean(x * x, axis=-1, keepdims=True) + NORM_EPS) * g


def _sigmoid(x):
    return 1.0 / (1.0 + jnp.exp(-x))


def _params(*sem):
    return pltpu.CompilerParams(dimension_semantics=sem, vmem_limit_bytes=VMEM_LIMIT_BYTES)


def _resident(shape):
    nd = len(shape)
    return pl.BlockSpec(shape, lambda *_: (0,) * nd, pipeline_mode=pl.Buffered(1))


def _cast_jobs(srcs, layer, step_of, n_steps):
    in_specs, out_specs, out_shapes, n_blocks = [], [], [], []
    for src in srcs:
        _, r, c = src.shape
        rows = next(m for m in range(BF16_ROWS, r + 1, BF16_ROWS) if r % m == 0 and r // m <= n_steps)
        nb = r // rows
        in_specs.append(pl.BlockSpec(
            (None, rows, c), lambda *g, nb=nb: (layer, jnp.minimum(step_of(*g), nb - 1), 0)))
        out_specs.append(pl.BlockSpec(
            (rows, c), lambda *g, nb=nb: (jnp.minimum(step_of(*g), nb - 1), 0)))
        out_shapes.append(jax.ShapeDtypeStruct((r, c), BF16))
        n_blocks.append(nb)
    return in_specs, out_specs, out_shapes, n_blocks


def _run_casts(step, src_refs, dst_refs, n_blocks):
    for src, dst, nb in zip(src_refs, dst_refs, n_blocks):
        @pl.when(step < nb)
        def _(src=src, dst=dst):
            dst[...] = src[...].astype(BF16)


def _ffn_tile(x, pre, post, wg_ref, wu_ref, wd_ref, acc_ref, fc):
    tm = x.shape[0]
    d_ff = wg_ref.shape[1]
    slabs = [slice(r, r + FFN_ROWS) for r in range(0, tm, FFN_ROWS)]
    chunks = [(c0, min(c0 + fc, d_ff)) for c0 in range(0, d_ff, fc)]

    def hidden(hh, c0, c1):
        g = _dot(hh, wg_ref[:, c0:c1])
        return (g * _sigmoid(g) * _dot(hh, wu_ref[:, c0:c1])).astype(BF16)

    hs = [_rms(x[r], pre).astype(BF16) for r in slabs]
    h = jnp.concatenate(hs, axis=0)
    out = None
    for n, (c0, c1) in enumerate(chunks):
        if n == 0:
            a = jnp.concatenate([hidden(hh, c0, c1) for hh in hs], axis=0)
        else:
            a = hidden(h, c0, c1)
        if n == len(chunks) - 1:
            out = jnp.concatenate(
                [x[r] + 0.5 * _rms(acc_ref[r, :] + _dot(a[r], wd_ref[c0:c1, :]), post) for r in slabs], axis=0)
        elif n == 0:
            acc_ref[...] = _dot(a, wd_ref[c0:c1, :])
        else:
            acc_ref[...] += _dot(a, wd_ref[c0:c1, :])
    return out


def _ffn_body(x_ref, pre_ref, post_ref, wg_ref, wu_ref, wd_ref, wint_ref, o_ref, wt_ref, acc_ref, *, fc):
    o_ref[...] = _ffn_tile(x_ref[...], pre_ref[...], post_ref[...], wg_ref, wu_ref, wd_ref, acc_ref, fc)
    wt_ref[...] = wint_ref[...].astype(BF16)


def _ffn(x, pre, post, wg, wu, wd, w_in_t, layer, *, tm=512, fc=512):
    t, d = x.shape
    n_steps = t // tm
    n_in = w_in_t.shape[1]
    rows = -(-n_in // (n_steps * BF16_ROWS)) * BF16_ROWS
    last = -(-n_in // rows) - 1
    row = pl.BlockSpec((tm, d), lambda i: (i, 0))
    return pl.pallas_call(
        functools.partial(_ffn_body, fc=fc),
        grid=(n_steps,),
        in_specs=[row, _resident((1, d)), _resident((1, d)),
                  _resident(wg.shape), _resident(wu.shape), _resident(wd.shape),
                  pl.BlockSpec((None, rows, d), lambda i: (layer, jnp.minimum(i, last), 0))],
        out_specs=[row, pl.BlockSpec((rows, d), lambda i: (jnp.minimum(i, last), 0))],
        out_shape=[jax.ShapeDtypeStruct((t, d), F32), jax.ShapeDtypeStruct((n_in, d), BF16)],
        scratch_shapes=[pltpu.VMEM((tm, d), F32)],
        compiler_params=_params("arbitrary"),
        name="ffn",
    )(x, pre.reshape(1, d), post.reshape(1, d), wg, wu, wd, w_in_t)


def _conv_tile(ext_ref, shift, w, bias, gain, beta, o_ref, *, tm, rb, halo):
    ne = rb + halo
    first = halo - (CONV_WIDTH - 1)

    def block(r0):
        shifted = _dot(shift, ext_ref[r0:r0 + ne, :])
        acc = bias
        for j in range(CONV_WIDTH):
            b = (first + j) % SUBLANES
            start = b * ne + (first + j - b)
            acc = acc + w[j:j + 1, :] * shifted[start:start + rb]
        mu = jnp.mean(acc, axis=-1, keepdims=True)
        yc = acc - mu
        var = jnp.mean(yc * yc, axis=-1, keepdims=True)
        yn = yc * lax.rsqrt(var + NORM_EPS) * gain + beta
        o_ref[0, r0:r0 + rb, :] = (yn * _sigmoid(yn)).astype(BF16)

    return [functools.partial(block, r0) for r0 in range(0, tm, rb)]


def _gla_tile(q, k, v, r, la, gain, state_ref, o_ref, *, tm):
    ch = GLA_CHUNK
    rr = lax.broadcasted_iota(jnp.int32, (ch, ch), 0)
    cc = lax.broadcasted_iota(jnp.int32, (ch, ch), 1)
    later = jnp.where(cc > rr, 1.0, 0.0).astype(BF16)
    key_head = lax.broadcasted_iota(jnp.int32, (1, GLA_KEY_DIM), 1) // GLA_HEAD_K

    n_chunks = tm // ch
    carry = [state_ref[...]]

    def chunk(n):
        state = carry[0]
        rows = slice(n * ch, (n + 1) * ch)
        la_c = la[rows]
        la_hi = la_c.astype(BF16)
        la_lo = (la_c - la_hi.astype(F32)).astype(BF16)
        decay_to_end = _dot(later, la_hi) + _dot(later, la_lo)
        chunk_decay = jnp.exp(decay_to_end[0:1, :] + la_c[0:1, :])
        k_dec = (k[rows].astype(F32) * jnp.exp(decay_to_end)).astype(BF16)
        state = state * chunk_decay + _dot_tn(v[rows], k_dec)
        st = state.astype(BF16)
        q_c = q[rows]
        normed = []
        for h in range(GLA_HEADS):
            qh = jnp.where(key_head == h, q_c, jnp.zeros_like(q_c))
            oh = _dot_nt(qh, st[h * GLA_HEAD_V:(h + 1) * GLA_HEAD_V, :])
            normed.append(oh * lax.rsqrt(jnp.mean(oh * oh, axis=-1, keepdims=True) + NORM_EPS))
        rg = r[rows].astype(F32)
        o_ref[0, rows, :] = (jnp.concatenate(normed, axis=1) * gain * (rg * _sigmoid(rg))).astype(BF16)
        carry[0] = state
        if n == n_chunks - 1:
            state_ref[...] = state

    return [functools.partial(chunk, n) for n in range(n_chunks)]


def _mixer_in_body(*refs, tm, rb, halo, n_blocks):
    n_cast = len(n_blocks)
    (x_ref, pre_ref, wt_ref, walpha_ref, balpha_ref,
     shift_ref, cw_ref, cb_ref, cg_ref, cbeta_ref, gg_ref) = refs[:11]
    cast_src = refs[11:11 + n_cast]
    q_ref, k_ref, v_ref, gates_ref, cv_ref, gl_ref = refs[11 + n_cast:17 + n_cast]
    cast_dst = refs[17 + n_cast:17 + 2 * n_cast]
    ext_ref, state_ref = refs[17 + 2 * n_cast:]
    _run_casts(pl.program_id(0) * pl.num_programs(1) + pl.program_id(1), cast_src, cast_dst, n_blocks)
    w = BRANCH_WIDTH
    dk = GLA_KEY_DIM

    @pl.when(pl.program_id(1) == 0)
    def _():
        ext_ref[0:halo, :] = jnp.zeros((halo, w), BF16)
        state_ref[...] = jnp.zeros_like(state_ref)

    @pl.when(pl.program_id(1) > 0)
    def _():
        ext_ref[0:halo, :] = ext_ref[tm:tm + halo, :]

    h = _rms(x_ref[0], pre_ref[...]).astype(BF16)
    o_conv = 3 * w
    o_gla = o_conv + 2 * w
    o_lr = o_gla + 2 * dk + 2 * w
    o_gate = o_lr + GLA_GATE_RANK

    qkv = _dot_nt(h, wt_ref[:o_conv, :])
    q_ref[0] = (qkv[:, :w] * (SB_HEAD_DIM ** -0.5)).astype(BF16)
    k_ref[0] = qkv[:, w:2 * w].astype(BF16)
    v_ref[0] = qkv[:, 2 * w:].astype(BF16)

    c = _dot_nt(h, wt_ref[o_conv:o_gla, :])
    ext_ref[halo:halo + tm, :] = (c[:, :w] * _sigmoid(c[:, w:])).astype(BF16)
    tasks = _conv_tile(ext_ref, shift_ref[...], cw_ref[...], cb_ref[...], cg_ref[...], cbeta_ref[...], cv_ref,
                       tm=tm, rb=rb, halo=halo)

    g4 = _dot_nt(h, wt_ref[o_gla:o_lr, :])
    lane = lax.broadcasted_iota(jnp.int32, (1, LANES), 1)
    lr = jnp.where(lane < GLA_GATE_RANK, _dot_nt(h, wt_ref[o_lr:o_lr + LANES, :]), 0.0)
    a = jnp.dot(lr, walpha_ref[...], preferred_element_type=F32,
                precision=lax.Precision.HIGHEST) + balpha_ref[...]
    la = (jnp.minimum(a, 0.0) - jnp.log(1.0 + jnp.exp(-jnp.abs(a)))) * (1.0 / GLA_GATE_TAU)
    tasks += _gla_tile((g4[:, :dk] * (GLA_HEAD_K ** -0.5)).astype(BF16), g4[:, dk:2 * dk].astype(BF16),
                       g4[:, 2 * dk:2 * dk + w].astype(BF16), g4[:, 2 * dk + w:].astype(BF16), la,
                       gg_ref[...], state_ref, gl_ref, tm=tm)

    n_gate = gates_ref.shape[2]
    gate_cols = [(c0, min(c0 + GATE_CHUNK, n_gate)) for c0 in range(0, n_gate, GATE_CHUNK)]
    per = -(-len(tasks) // len(gate_cols))
    for j, (c0, c1) in enumerate(gate_cols):
        gates_ref[0, :, c0:c1] = _sigmoid(_dot_nt(h, wt_ref[o_gate + c0:o_gate + c1, :])).astype(BF16)
        for task in tasks[j * per:(j + 1) * per]:
            task()


def _mixer_in(x, pre, w_t, walpha, balpha, conv_w, conv_b, conv_g,
              conv_beta, gla_g, cast_srcs, layer, *, tm=512, rb=128, halo=32):
    b, s, d = x.shape
    w = BRANCH_WIDTH
    dk = GLA_KEY_DIM
    n_gate = N_BRANCHES * d
    assert w_t.shape[0] == 5 * w + 2 * dk + 2 * w + GLA_GATE_RANK + n_gate
    n_seq = s // tm
    c_in, c_out, c_shapes, n_blocks = _cast_jobs(cast_srcs, layer, lambda bi, i: bi * n_seq + i, b * n_seq)
    assert halo >= CONV_WIDTH - 1 and halo % (2 * SUBLANES) == 0 and tm % rb == 0 and s % tm == 0
    ne = rb + halo
    m = jnp.arange(ne)
    shift = jnp.concatenate([(m[None, :] == m[:, None] + j) for j in range(SUBLANES)], axis=0).astype(BF16)

    def row(n):
        return pl.BlockSpec((1, tm, n), lambda bi, i: (bi, i, 0))

    def out(n):
        return jax.ShapeDtypeStruct((b, s, n), BF16)

    vec = _resident((1, w))
    return pl.pallas_call(
        functools.partial(_mixer_in_body, tm=tm, rb=rb, halo=halo, n_blocks=tuple(n_blocks)),
        grid=(b, n_seq),
        in_specs=[row(d), _resident((1, d)), _resident(w_t.shape),
                  _resident(walpha.shape), _resident((1, dk)), _resident(shift.shape),
                  _resident(conv_w.shape), vec, vec, vec, vec] + c_in,
        out_specs=[row(w), row(w), row(w), row(n_gate), row(w), row(w)] + c_out,
        out_shape=[out(w), out(w), out(w), out(n_gate), out(w), out(w)] + c_shapes,
        scratch_shapes=[pltpu.VMEM((halo + tm, w), BF16), pltpu.VMEM((w, dk), F32)],
        compiler_params=_params("arbitrary", "arbitrary"),
        name="mixer_in",
    )(x, pre.reshape(1, d), w_t, walpha, balpha.reshape(1, dk), shift,
      conv_w, conv_b.reshape(1, w), conv_g.reshape(1, w), conv_beta.reshape(1, w), gla_g.reshape(1, w),
      *cast_srcs)


def _attn_body(q_ref, k_ref, v_ref, o_ref, stick_ref, acc_ref, *, tb, nq, window, band):
    i = pl.program_id(1)
    n_pairs = q_ref.shape[2] // LANES
    hb = tb // 2
    wk = window * tb
    lane = lax.broadcasted_iota(jnp.int32, (1, LANES), 1)
    head0 = lane < SB_HEAD_DIM

    def pair(ref, r0, n, p):
        return ref[0, pl.ds(r0, n), p * LANES:(p + 1) * LANES]

    def stacked_q(p, r0, n):
        q = pair(q_ref, r0, n, p)
        zero = jnp.zeros_like(q)
        return jnp.concatenate([jnp.where(head0, q, zero), jnp.where(head0, zero, q)], axis=0)

    def log_terms(z):
        lg = jnp.log(1.0 + jnp.exp(-jnp.abs(z)))
        sp = jnp.maximum(z, 0.0) + lg
        return sp, z - sp

    def write_out(acc, r0, n, p):
        o_ref[0, pl.ds(r0, n), p * LANES:(p + 1) * LANES] = jnp.where(head0, acc[:n], acc[n:]).astype(BF16)

    def tri(n):
        return jnp.where(lax.broadcasted_iota(jnp.int32, (n, n), 0) > lax.broadcasted_iota(jnp.int32, (n, n), 1),
                         1.0, 0.0)

    def query_row(n_rows, n_cols):
        r = lax.broadcasted_iota(jnp.int32, (2 * n_rows, n_cols), 0)
        return jnp.where(r >= n_rows, r - n_rows, r)

    def masked_pass(qs_list, w0_list, width, mask):
        sps, lss = [], []
        for (p, qs), w0 in zip(qs_list, w0_list):
            sp, ls = log_terms(_dot_nt(qs, pair(k_ref, w0, width, p)))
            sps.append(mask(sp))
            lss.append(ls)
        later = _dot(jnp.concatenate(sps, axis=0).astype(BF16), tri(width).astype(BF16))
        n = later.shape[0] // len(sps)
        out = []
        for e, ((p, _), w0) in enumerate(zip(qs_list, w0_list)):
            lt = later[e * n:(e + 1) * n]
            wgt = mask(jnp.exp(lss[e] - lt))
            out.append((_dot(wgt.astype(BF16), pair(v_ref, w0, width, p)), lt[:, 0:1] + sps[e][:, 0:1]))
        return out

    def last_tile_mask(keep, width):
        def mask(a):
            return jnp.concatenate([a[:, :width - LANES], jnp.where(keep, a[:, width - LANES:], 0.0)], axis=1)
        return mask

    def band_pass():
        keep = lax.broadcasted_iota(jnp.int32, (2 * hb, LANES), 1) < query_row(hb, LANES) + (LANES - hb)
        entries, starts, rows0 = [], [], []
        for half in range(2 * nq):
            r0 = half * hb
            w0 = pl.multiple_of(i * (nq * tb) + (r0 + hb - band), hb)
            for p in range(n_pairs):
                entries.append((p, stacked_q(p, r0, hb)))
                starts.append(w0)
                rows0.append(r0)
        res = masked_pass(entries, starts, band, last_tile_mask(keep, band))
        mins = [None] * nq
        for (acc, stick), (p, _), r0 in zip(res, entries, rows0):
            write_out(acc, r0, hb, p)
            blk = r0 // tb
            mins[blk] = stick if mins[blk] is None else jnp.minimum(mins[blk], stick)
        return [jnp.min(m) for m in mins]

    def window_pass(blk_r0, w0, keep, aligned):
        if aligned:
            mask = last_tile_mask(keep, wk)
        else:
            def mask(a):
                return jnp.where(keep, a, 0.0)
        entries = [(p, stacked_q(p, blk_r0, tb)) for p in range(n_pairs)]
        res = masked_pass(entries, [w0] * n_pairs, wk, mask)
        for (acc, _), p in zip(res, range(n_pairs)):
            write_out(acc, blk_r0, tb, p)
        return res

    def block_step(j0, p, blk_r0, stick, acc, suffix):
        sp, ls = log_terms(_dot_nt(stacked_q(p, blk_r0, tb), pair(k_ref, j0, tb, p)))
        sums = _dot(sp.astype(BF16), suffix)
        wgt = jnp.exp(ls - sums[:, :tb] - stick)
        return stick + sums[:, tb:], acc + _dot(wgt.astype(BF16), pair(v_ref, j0, tb, p))

    def min_stick(sticks):
        m = sticks[0]
        for s in sticks[1:]:
            m = jnp.minimum(m, s)
        return jnp.min(m)

    def exact_block(blk):
        blk_r0 = blk * tb
        ib = i * nq + blk
        keep = lax.broadcasted_iota(jnp.int32, (2 * tb, LANES), 1) < query_row(tb, LANES) + (LANES - tb)
        res = window_pass(blk_r0, pl.multiple_of((ib - (window - 1)) * tb, tb), keep, True)
        for p in range(n_pairs):
            acc_ref[p] = res[p][0]
            stick_ref[p] = jnp.broadcast_to(res[p][1], (2 * tb, tb))
        suffix = jnp.concatenate([tri(tb), jnp.ones((tb, tb), F32)], axis=1).astype(BF16)

        def cond(st):
            return jnp.logical_and(st[0] >= 0, st[1] > 0)

        def body(st):
            j0 = pl.multiple_of(st[0] * tb, tb)
            new = []
            for p in range(n_pairs):
                stick, acc = block_step(j0, p, blk_r0, stick_ref[p], acc_ref[p], suffix)
                stick_ref[p] = stick
                acc_ref[p] = acc
                new.append(stick)
            return st[0] - 1, (min_stick(new) < DEAD_STICK).astype(jnp.int32)

        alive = (min_stick([r[1] for r in res]) < DEAD_STICK).astype(jnp.int32)
        lax.while_loop(cond, body, (ib - window, alive))
        for p in range(n_pairs):
            write_out(acc_ref[p], blk_r0, tb, p)

    @pl.when(i > 0)
    def _():
        for blk, lowest in enumerate(band_pass()):
            pl.when(lowest < DEAD_STICK)(functools.partial(exact_block, blk))

    @pl.when(i == 0)
    def _():
        for blk in range(nq):
            keep = lax.broadcasted_iota(jnp.int32, (2 * tb, wk), 1) < query_row(tb, wk) + blk * tb
            window_pass(blk * tb, 0, keep, False)


def _attn(q, k, v, *, tb=128, nq=2, window=3, band=256):
    b, s, w = q.shape
    tq = nq * tb
    assert s % tq == 0 and w % LANES == 0 and tq <= window * tb <= s
    assert band % LANES == 0 and tb // 2 <= LANES and band - tb // 2 <= tq and (window - 1) * tb <= tq
    n_pairs = w // LANES
    qspec = pl.BlockSpec((1, tq, w), lambda bi, i: (bi, i, 0))
    kvspec = pl.BlockSpec((1, s, w), lambda bi, i: (bi, 0, 0))
    return pl.pallas_call(
        functools.partial(_attn_body, tb=tb, nq=nq, window=window, band=band),
        grid=(b, s // tq),
        in_specs=[qspec, kvspec, kvspec],
        out_specs=qspec,
        out_shape=jax.ShapeDtypeStruct((b, s, w), BF16),
        scratch_shapes=[pltpu.VMEM((n_pairs, 2 * tb, tb), F32), pltpu.VMEM((n_pairs, 2 * tb, LANES), F32)],
        compiler_params=_params("parallel", "parallel"),
        name="stickbreak_attn",
    )(q, k, v)


def _merge_ffn_body(*refs, fc, n_blocks):
    n_cast = len(n_blocks)
    (x_ref, sb_ref, cv_ref, gl_ref, gates_ref, wb_ref, wo_ref, mpost_ref,
     pre_ref, post_ref, wg_ref, wu_ref, wd_ref) = refs[:13]
    cast_src = refs[13:13 + n_cast]
    o_ref = refs[13 + n_cast]
    cast_dst = refs[14 + n_cast:14 + 2 * n_cast]
    acc_ref = refs[14 + 2 * n_cast]
    _run_casts(pl.program_id(0), cast_src, cast_dst, n_blocks)
    d = x_ref.shape[1]
    w = sb_ref.shape[1]
    merged = gates_ref[:, 0:d].astype(F32) * _dot(sb_ref[...], wb_ref[0:w, :])
    merged = merged + gates_ref[:, d:2 * d].astype(F32) * _dot(cv_ref[...], wb_ref[w:2 * w, :])
    merged = merged + gates_ref[:, 2 * d:3 * d].astype(F32) * _dot(gl_ref[...], wb_ref[2 * w:3 * w, :])
    m = _dot(merged.astype(BF16), wo_ref[...])
    x = x_ref[...] + _rms(m, mpost_ref[...])
    o_ref[...] = _ffn_tile(x, pre_ref[...], post_ref[...], wg_ref, wu_ref, wd_ref, acc_ref, fc)


def _merge_ffn(x, sb, cv, gl, gates, wb, wo, mpost, pre, post, wg, wu, wd, cast_srcs, layer,
               *, tm=512, fc=512):
    t, d = x.shape
    w = sb.shape[1]
    c_in, c_out, c_shapes, n_blocks = _cast_jobs(cast_srcs, layer, lambda i: i, t // tm)

    def row(n):
        return pl.BlockSpec((tm, n), lambda i: (i, 0))

    vec = _resident((1, d))
    return pl.pallas_call(
        functools.partial(_merge_ffn_body, fc=fc, n_blocks=tuple(n_blocks)),
        grid=(t // tm,),
        in_specs=[row(d), row(w), row(w), row(w), row(3 * d), _resident(wb.shape),
                  _resident(wo.shape), vec, vec, vec,
                  _resident(wg.shape), _resident(wu.shape), _resident(wd.shape)] + c_in,
        out_specs=[row(d)] + c_out,
        out_shape=[jax.ShapeDtypeStruct((t, d), F32)] + c_shapes,
        scratch_shapes=[pltpu.VMEM((tm, d), F32)],
        compiler_params=_params("arbitrary"),
        name="merge_ffn",
    )(x, sb, cv, gl, gates, wb, wo, mpost.reshape(1, d), pre.reshape(1, d), post.reshape(1, d),
      wg, wu, wd, *cast_srcs)


def kernel(x, norm_pre, norm_post, ffn1_w_gate, ffn1_w_up, ffn1_w_down, ffn2_w_gate, ffn2_w_up,
           ffn2_w_down, w_in, conv_w, conv_b, conv_ln_g, conv_ln_b, gla_w_alpha, gla_b_alpha,
           gla_norm_g, w_branch, w_out):
    b, s, d = x.shape
    t = b * s
    n_layers = norm_pre.shape[0]
    walpha = jnp.pad(gla_w_alpha, ((0, 0), (0, LANES - GLA_GATE_RANK), (0, 0)))
    w_branch2 = w_branch.reshape(n_layers, N_BRANCHES * BRANCH_WIDTH, d)
    w_in_t = jnp.swapaxes(w_in, 1, 2)
    ffn1_srcs = [ffn1_w_gate, ffn1_w_up, ffn1_w_down]
    mix_srcs = [ffn2_w_gate, ffn2_w_up, ffn2_w_down, w_branch2, w_out]

    def flat(a):
        return a.reshape(t, a.shape[2])

    ffn1 = [a[0].astype(BF16) for a in ffn1_srcs]
    xf = x.reshape(t, d)
    for l in range(n_layers):
        xf, w_t = _ffn(xf, norm_pre[l, 0], norm_post[l, 0], *ffn1, w_in_t, l)
        q, k, v, gates, cv, gl, *mix_w = _mixer_in(
            xf.reshape(b, s, d), norm_pre[l, 1], w_t, walpha[l], gla_b_alpha[l],
            conv_w[l], conv_b[l], conv_ln_g[l], conv_ln_b[l], gla_norm_g[l], mix_srcs, l)
        sb = _attn(q, k, v)
        ffn2, (wb, wo) = mix_w[:3], mix_w[3:]
        nxt = ffn1_srcs if l + 1 < n_layers else []
        xf, *ffn1 = _merge_ffn(xf, flat(sb), flat(cv), flat(gl), flat(gates), wb, wo,
                               norm_post[l, 1], norm_pre[l, 2], norm_post[l, 2], *ffn2, nxt, l + 1)
    return xf.reshape(b, s, d)
```

```python
import functools

import jax
import jax.numpy as jnp
from jax import lax
from jax.experimental import pallas as pl
from jax.experimental.pallas import tpu as pltpu

NORM_EPS = 1e-6
BRANCH_WIDTH = 512
SB_HEAD_DIM = 64
CONV_WIDTH = 31
GLA_KEY_DIM = 256
GLA_HEADS = 4
GLA_HEAD_K = 64
GLA_HEAD_V = 128
GLA_GATE_RANK = 16
GLA_GATE_TAU = 16.0
GLA_CHUNK = 64
N_BRANCHES = 3
FFN_ROWS = 128
GATE_CHUNK = 256

DEAD_STICK = 105.0

LANES = 128
SUBLANES = 8
BF16_ROWS = 16
VMEM_LIMIT_BYTES = 56 * 1024 * 1024

F32 = jnp.float32
BF16 = jnp.bfloat16


def _dot(a, b):
    return jnp.dot(a, b, preferred_element_type=F32)


def _dot_nt(a, b):
    return lax.dot_general(a, b, (((1,), (1,)), ((), ())), preferred_element_type=F32)


def _dot_tn(a, b):
    return lax.dot_general(a, b, (((0,), (0,)), ((), ())), preferred_element_type=F32)


def _rms(x, g):
    return x * lax.rsqrt(jnp.mean(x * x, axis=-1, keepdims=True) + NORM_EPS) * g


def _sigmoid(x):
    return 1.0 / (1.0 + jnp.exp(-x))


def _params(*sem):
    return pltpu.CompilerParams(dimension_semantics=sem, vmem_limit_bytes=VMEM_LIMIT_BYTES)


def _resident(shape):
    nd = len(shape)
    return pl.BlockSpec(shape, lambda *_: (0,) * nd, pipeline_mode=pl.Buffered(1))


def _cast_jobs(srcs, layer, step_of, n_steps):
    in_specs, out_specs, out_shapes, n_blocks = [], [], [], []
    for src in srcs:
        _, r, c = src.shape
        rows = next(m for m in range(BF16_ROWS, r + 1, BF16_ROWS) if r % m == 0 and r // m <= n_steps)
        nb = r // rows
        in_specs.append(pl.BlockSpec(
            (None, rows, c), lambda *g, nb=nb: (layer, jnp.minimum(step_of(*g), nb - 1), 0)))
        out_specs.append(pl.BlockSpec(
            (rows, c), lambda *g, nb=nb: (jnp.minimum(step_of(*g), nb - 1), 0)))
        out_shapes.append(jax.ShapeDtypeStruct((r, c), BF16))
        n_blocks.append(nb)
    return in_specs, out_specs, out_shapes, n_blocks


def _run_casts(step, src_refs, dst_refs, n_blocks):
    for src, dst, nb in zip(src_refs, dst_refs, n_blocks):
        @pl.when(step < nb)
        def _(src=src, dst=dst):
            dst[...] = src[...].astype(BF16)


def _ffn_tile(x, pre, post, wg_ref, wu_ref, wd_ref, acc_ref, fc):
    tm = x.shape[0]
    d_ff = wg_ref.shape[1]
    slabs = [slice(r, r + FFN_ROWS) for r in range(0, tm, FFN_ROWS)]
    chunks = [(c0, min(c0 + fc, d_ff)) for c0 in range(0, d_ff, fc)]

    def hidden(hh, c0, c1):
        g = _dot(hh, wg_ref[:, c0:c1])
        return (g * _sigmoid(g) * _dot(hh, wu_ref[:, c0:c1])).astype(BF16)

    hs = [_rms(x[r], pre).astype(BF16) for r in slabs]
    h = jnp.concatenate(hs, axis=0)
    out = None
    for n, (c0, c1) in enumerate(chunks):
        if n == 0:
            a = jnp.concatenate([hidden(hh, c0, c1) for hh in hs], axis=0)
        else:
            a = hidden(h, c0, c1)
        if n == len(chunks) - 1:
            out = jnp.concatenate(
                [x[r] + 0.5 * _rms(acc_ref[r, :] + _dot(a[r], wd_ref[c0:c1, :]), post) for r in slabs], axis=0)
        elif n == 0:
            acc_ref[...] = _dot(a, wd_ref[c0:c1, :])
        else:
            acc_ref[...] += _dot(a, wd_ref[c0:c1, :])
    return out


def _ffn_body(x_ref, pre_ref, post_ref, wg_ref, wu_ref, wd_ref, wint_ref, o_ref, wt_ref, acc_ref, *, fc):
    o_ref[...] = _ffn_tile(x_ref[...], pre_ref[...], post_ref[...], wg_ref, wu_ref, wd_ref, acc_ref, fc)
    wt_ref[...] = wint_ref[...].astype(BF16)


def _ffn(x, pre, post, wg, wu, wd, w_in_t, layer, *, tm=512, fc=512):
    t, d = x.shape
    n_steps = t // tm
    n_in = w_in_t.shape[1]
    rows = -(-n_in // (n_steps * BF16_ROWS)) * BF16_ROWS
    last = -(-n_in // rows) - 1
    row = pl.BlockSpec((tm, d), lambda i: (i, 0))
    return pl.pallas_call(
        functools.partial(_ffn_body, fc=fc),
        grid=(n_steps,),
        in_specs=[row, _resident((1, d)), _resident((1, d)),
                  _resident(wg.shape), _resident(wu.shape), _resident(wd.shape),
                  pl.BlockSpec((None, rows, d), lambda i: (layer, jnp.minimum(i, last), 0))],
        out_specs=[row, pl.BlockSpec((rows, d), lambda i: (jnp.minimum(i, last), 0))],
        out_shape=[jax.ShapeDtypeStruct((t, d), F32), jax.ShapeDtypeStruct((n_in, d), BF16)],
        scratch_shapes=[pltpu.VMEM((tm, d), F32)],
        compiler_params=_params("arbitrary"),
        name="ffn",
    )(x, pre.reshape(1, d), post.reshape(1, d), wg, wu, wd, w_in_t)


def _conv_tile(ext_ref, shift, w, bias, gain, beta, o_ref, *, tm, rb, halo):
    ne = rb + halo
    first = halo - (CONV_WIDTH - 1)

    def block(r0):
        shifted = _dot(shift, ext_ref[r0:r0 + ne, :])
        acc = bias
        for j in range(CONV_WIDTH):
            b = (first + j) % SUBLANES
            start = b * ne + (first + j - b)
            acc = acc + w[j:j + 1, :] * shifted[start:start + rb]
        mu = jnp.mean(acc, axis=-1, keepdims=True)
        yc = acc - mu
        var = jnp.mean(yc * yc, axis=-1, keepdims=True)
        yn = yc * lax.rsqrt(var + NORM_EPS) * gain + beta
        o_ref[0, r0:r0 + rb, :] = (yn * _sigmoid(yn)).astype(BF16)

    return [functools.partial(block, r0) for r0 in range(0, tm, rb)]


def _gla_tile(q, k, v, r, la, gain, state_ref, o_ref, *, tm):
    ch = GLA_CHUNK
    rr = lax.broadcasted_iota(jnp.int32, (ch, ch), 0)
    cc = lax.broadcasted_iota(jnp.int32, (ch, ch), 1)
    later = jnp.where(cc > rr, 1.0, 0.0).astype(BF16)
    key_head = lax.broadcasted_iota(jnp.int32, (1, GLA_KEY_DIM), 1) // GLA_HEAD_K

    n_chunks = tm // ch
    carry = [state_ref[...]]

    def chunk(n):
        state = carry[0]
        rows = slice(n * ch, (n + 1) * ch)
        la_c = la[rows]
        la_hi = la_c.astype(BF16)
        la_lo = (la_c - la_hi.astype(F32)).astype(BF16)
        decay_to_end = _dot(later, la_hi) + _dot(later, la_lo)
        chunk_decay = jnp.exp(decay_to_end[0:1, :] + la_c[0:1, :])
        k_dec = (k[rows].astype(F32) * jnp.exp(decay_to_end)).astype(BF16)
        state = state * chunk_decay + _dot_tn(v[rows], k_dec)
        st = state.astype(BF16)
        q_c = q[rows]
        normed = []
        for h in range(GLA_HEADS):
            qh = jnp.where(key_head == h, q_c, jnp.zeros_like(q_c))
            oh = _dot_nt(qh, st[h * GLA_HEAD_V:(h + 1) * GLA_HEAD_V, :])
            normed.append(oh * lax.rsqrt(jnp.mean(oh * oh, axis=-1, keepdims=True) + NORM_EPS))
        rg = r[rows].astype(F32)
        o_ref[0, rows, :] = (jnp.concatenate(normed, axis=1) * gain * (rg * _sigmoid(rg))).astype(BF16)
        carry[0] = state
        if n == n_chunks - 1:
            state_ref[...] = state

    return [functools.partial(chunk, n) for n in range(n_chunks)]


def _mixer_in_body(*refs, tm, rb, halo, n_blocks):
    n_cast = len(n_blocks)
    (x_ref, pre_ref, wt_ref, walpha_ref, balpha_ref,
     shift_ref, cw_ref, cb_ref, cg_ref, cbeta_ref, gg_ref) = refs[:11]
    cast_src = refs[11:11 + n_cast]
    q_ref, k_ref, v_ref, gates_ref, cv_ref, gl_ref = refs[11 + n_cast:17 + n_cast]
    cast_dst = refs[17 + n_cast:17 + 2 * n_cast]
    ext_ref, state_ref = refs[17 + 2 * n_cast:]
    _run_casts(pl.program_id(0) * pl.num_programs(1) + pl.program_id(1), cast_src, cast_dst, n_blocks)
    w = BRANCH_WIDTH
    dk = GLA_KEY_DIM

    @pl.when(pl.program_id(1) == 0)
    def _():
        ext_ref[0:halo, :] = jnp.zeros((halo, w), BF16)
        state_ref[...] = jnp.zeros_like(state_ref)

    @pl.when(pl.program_id(1) > 0)
    def _():
        ext_ref[0:halo, :] = ext_ref[tm:tm + halo, :]

    h = _rms(x_ref[0], pre_ref[...]).astype(BF16)
    o_conv = 3 * w
    o_gla = o_conv + 2 * w
    o_lr = o_gla + 2 * dk + 2 * w
    o_gate = o_lr + GLA_GATE_RANK

    qkv = _dot_nt(h, wt_ref[:o_conv, :])
    q_ref[0] = (qkv[:, :w] * (SB_HEAD_DIM ** -0.5)).astype(BF16)
    k_ref[0] = qkv[:, w:2 * w].astype(BF16)
    v_ref[0] = qkv[:, 2 * w:].astype(BF16)

    c = _dot_nt(h, wt_ref[o_conv:o_gla, :])
    ext_ref[halo:halo + tm, :] = (c[:, :w] * _sigmoid(c[:, w:])).astype(BF16)
    tasks = _conv_tile(ext_ref, shift_ref[...], cw_ref[...], cb_ref[...], cg_ref[...], cbeta_ref[...], cv_ref,
                       tm=tm, rb=rb, halo=halo)

    g4 = _dot_nt(h, wt_ref[o_gla:o_lr, :])
    lane = lax.broadcasted_iota(jnp.int32, (1, LANES), 1)
    lr = jnp.where(lane < GLA_GATE_RANK, _dot_nt(h, wt_ref[o_lr:o_lr + LANES, :]), 0.0)
    a = jnp.dot(lr, walpha_ref[...], preferred_element_type=F32,
                precision=lax.Precision.HIGHEST) + balpha_ref[...]
    la = (jnp.minimum(a, 0.0) - jnp.log(1.0 + jnp.exp(-jnp.abs(a)))) * (1.0 / GLA_GATE_TAU)
    tasks += _gla_tile((g4[:, :dk] * (GLA_HEAD_K ** -0.5)).astype(BF16), g4[:, dk:2 * dk].astype(BF16),
                       g4[:, 2 * dk:2 * dk + w].astype(BF16), g4[:, 2 * dk + w:].astype(BF16), la,
                       gg_ref[...], state_ref, gl_ref, tm=tm)

    n_gate = gates_ref.shape[2]
    gate_cols = [(c0, min(c0 + GATE_CHUNK, n_gate)) for c0 in range(0, n_gate, GATE_CHUNK)]
    per = -(-len(tasks) // len(gate_cols))
    for j, (c0, c1) in enumerate(gate_cols):
        gates_ref[0, :, c0:c1] = _sigmoid(_dot_nt(h, wt_ref[o_gate + c0:o_gate + c1, :])).astype(BF16)
        for task in tasks[j * per:(j + 1) * per]:
            task()


def _mixer_in(x, pre, w_t, walpha, balpha, conv_w, conv_b, conv_g,
              conv_beta, gla_g, cast_srcs, layer, *, tm=512, rb=128, halo=32):
    b, s, d = x.shape
    w = BRANCH_WIDTH
    dk = GLA_KEY_DIM
    n_gate = N_BRANCHES * d
    assert w_t.shape[0] == 5 * w + 2 * dk + 2 * w + GLA_GATE_RANK + n_gate
    n_seq = s // tm
    c_in, c_out, c_shapes, n_blocks = _cast_jobs(cast_srcs, layer, lambda bi, i: bi * n_seq + i, b * n_seq)
    assert halo >= CONV_WIDTH - 1 and halo % (2 * SUBLANES) == 0 and tm % rb == 0 and s % tm == 0
    ne = rb + halo
    m = jnp.arange(ne)
    shift = jnp.concatenate([(m[None, :] == m[:, None] + j) for j in range(SUBLANES)], axis=0).astype(BF16)

    def row(n):
        return pl.BlockSpec((1, tm, n), lambda bi, i: (bi, i, 0))

    def out(n):
        return jax.ShapeDtypeStruct((b, s, n), BF16)

    vec = _resident((1, w))
    return pl.pallas_call(
        functools.partial(_mixer_in_body, tm=tm, rb=rb, halo=halo, n_blocks=tuple(n_blocks)),
        grid=(b, n_seq),
        in_specs=[row(d), _resident((1, d)), _resident(w_t.shape),
                  _resident(walpha.shape), _resident((1, dk)), _resident(shift.shape),
                  _resident(conv_w.shape), vec, vec, vec, vec] + c_in,
        out_specs=[row(w), row(w), row(w), row(n_gate), row(w), row(w)] + c_out,
        out_shape=[out(w), out(w), out(w), out(n_gate), out(w), out(w)] + c_shapes,
        scratch_shapes=[pltpu.VMEM((halo + tm, w), BF16), pltpu.VMEM((w, dk), F32)],
        compiler_params=_params("arbitrary", "arbitrary"),
        name="mixer_in",
    )(x, pre.reshape(1, d), w_t, walpha, balpha.reshape(1, dk), shift,
      conv_w, conv_b.reshape(1, w), conv_g.reshape(1, w), conv_beta.reshape(1, w), gla_g.reshape(1, w),
      *cast_srcs)


def _attn_body(q_ref, k_ref, v_ref, o_ref, stick_ref, acc_ref, *, tb, nq, window, band):
    i = pl.program_id(1)
    n_pairs = q_ref.shape[2] // LANES
    hb = tb // 2
    wk = window * tb
    lane = lax.broadcasted_iota(jnp.int32, (1, LANES), 1)
    head0 = lane < SB_HEAD_DIM

    def pair(ref, r0, n, p):
        return ref[0, pl.ds(r0, n), p * LANES:(p + 1) * LANES]

    def stacked_q(p, r0, n):
        q = pair(q_ref, r0, n, p)
        zero = jnp.zeros_like(q)
        return jnp.concatenate([jnp.where(head0, q, zero), jnp.where(head0, zero, q)], axis=0)

    def log_terms(z):
        lg = jnp.log(1.0 + jnp.exp(-jnp.abs(z)))
        sp = jnp.maximum(z, 0.0) + lg
        return sp, z - sp

    def write_out(acc, r0, n, p):
        o_ref[0, pl.ds(r0, n), p * LANES:(p + 1) * LANES] = jnp.where(head0, acc[:n], acc[n:]).astype(BF16)

    def tri(n):
        return jnp.where(lax.broadcasted_iota(jnp.int32, (n, n), 0) > lax.broadcasted_iota(jnp.int32, (n, n), 1),
                         1.0, 0.0)

    def query_row(n_rows, n_cols):
        r = lax.broadcasted_iota(jnp.int32, (2 * n_rows, n_cols), 0)
        return jnp.where(r >= n_rows, r - n_rows, r)

    def masked_pass(qs_list, w0_list, width, mask):
        sps, lss = [], []
        for (p, qs), w0 in zip(qs_list, w0_list):
            sp, ls = log_terms(_dot_nt(qs, pair(k_ref, w0, width, p)))
            sps.append(mask(sp))
            lss.append(ls)
        later = _dot(jnp.concatenate(sps, axis=0).astype(BF16), tri(width).astype(BF16))
        n = later.shape[0] // len(sps)
        out = []
        for e, ((p, _), w0) in enumerate(zip(qs_list, w0_list)):
            lt = later[e * n:(e + 1) * n]
            wgt = mask(jnp.exp(lss[e] - lt))
            out.append((_dot(wgt.astype(BF16), pair(v_ref, w0, width, p)), lt[:, 0:1] + sps[e][:, 0:1]))
        return out

    def last_tile_mask(keep, width):
        def mask(a):
            return jnp.concatenate([a[:, :width - LANES], jnp.where(keep, a[:, width - LANES:], 0.0)], axis=1)
        return mask

    def band_pass():
        keep = lax.broadcasted_iota(jnp.int32, (2 * hb, LANES), 1) < query_row(hb, LANES) + (LANES - hb)
        entries, starts, rows0 = [], [], []
        for half in range(2 * nq):
            r0 = half * hb
            w0 = pl.multiple_of(i * (nq * tb) + (r0 + hb - band), hb)
            for p in range(n_pairs):
                entries.append((p, stacked_q(p, r0, hb)))
                starts.append(w0)
                rows0.append(r0)
        res = masked_pass(entries, starts, band, last_tile_mask(keep, band))
        mins = [None] * nq
        for (acc, stick), (p, _), r0 in zip(res, entries, rows0):
            write_out(acc, r0, hb, p)
            blk = r0 // tb
            mins[blk] = stick if mins[blk] is None else jnp.minimum(mins[blk], stick)
        return [jnp.min(m) for m in mins]

    def window_pass(blk_r0, w0, keep, aligned):
        if aligned:
            mask = last_tile_mask(keep, wk)
        else:
            def mask(a):
                return jnp.where(keep, a, 0.0)
        entries = [(p, stacked_q(p, blk_r0, tb)) for p in range(n_pairs)]
        res = masked_pass(entries, [w0] * n_pairs, wk, mask)
        for (acc, _), p in zip(res, range(n_pairs)):
            write_out(acc, blk_r0, tb, p)
        return res

    def block_step(j0, p, blk_r0, stick, acc, suffix):
        sp, ls = log_terms(_dot_nt(stacked_q(p, blk_r0, tb), pair(k_ref, j0, tb, p)))
        sums = _dot(sp.astype(BF16), suffix)
        wgt = jnp.exp(ls - sums[:, :tb] - stick)
        return stick + sums[:, tb:], acc + _dot(wgt.astype(BF16), pair(v_ref, j0, tb, p))

    def min_stick(sticks):
        m = sticks[0]
        for s in sticks[1:]:
            m = jnp.minimum(m, s)
        return jnp.min(m)

    def exact_block(blk):
        blk_r0 = blk * tb
        ib = i * nq + blk
        keep = lax.broadcasted_iota(jnp.int32, (2 * tb, LANES), 1) < query_row(tb, LANES) + (LANES - tb)
        res = window_pass(blk_r0, pl.multiple_of((ib - (window - 1)) * tb, tb), keep, True)
        for p in range(n_pairs):
            acc_ref[p] = res[p][0]
            stick_ref[p] = jnp.broadcast_to(res[p][1], (2 * tb, tb))
        suffix = jnp.concatenate([tri(tb), jnp.ones((tb, tb), F32)], axis=1).astype(BF16)

        def cond(st):
            return jnp.logical_and(st[0] >= 0, st[1] > 0)

        def body(st):
            j0 = pl.multiple_of(st[0] * tb, tb)
            new = []
            for p in range(n_pairs):
                stick, acc = block_step(j0, p, blk_r0, stick_ref[p], acc_ref[p], suffix)
                stick_ref[p] = stick
                acc_ref[p] = acc
                new.append(stick)
            return st[0] - 1, (min_stick(new) < DEAD_STICK).astype(jnp.int32)

        alive = (min_stick([r[1] for r in res]) < DEAD_STICK).astype(jnp.int32)
        lax.while_loop(cond, body, (ib - window, alive))
        for p in range(n_pairs):
            write_out(acc_ref[p], blk_r0, tb, p)

    @pl.when(i > 0)
    def _():
        for blk, lowest in enumerate(band_pass()):
            pl.when(lowest < DEAD_STICK)(functools.partial(exact_block, blk))

    @pl.when(i == 0)
    def _():
        for blk in range(nq):
            keep = lax.broadcasted_iota(jnp.int32, (2 * tb, wk), 1) < query_row(tb, wk) + blk * tb
            window_pass(blk * tb, 0, keep, False)


def _attn(q, k, v, *, tb=128, nq=4, window=4, band=256):
    b, s, w = q.shape
    tq = nq * tb
    assert s % tq == 0 and w % LANES == 0 and tq <= window * tb <= s
    assert band % LANES == 0 and tb // 2 <= LANES and band - tb // 2 <= tq and (window - 1) * tb <= tq
    n_pairs = w // LANES
    qspec = pl.BlockSpec((1, tq, w), lambda bi, i: (bi, i, 0))
    kvspec = pl.BlockSpec((1, s, w), lambda bi, i: (bi, 0, 0))
    return pl.pallas_call(
        functools.partial(_attn_body, tb=tb, nq=nq, window=window, band=band),
        grid=(b, s // tq),
        in_specs=[qspec, kvspec, kvspec],
        out_specs=qspec,
        out_shape=jax.ShapeDtypeStruct((b, s, w), BF16),
        scratch_shapes=[pltpu.VMEM((n_pairs, 2 * tb, tb), F32), pltpu.VMEM((n_pairs, 2 * tb, LANES), F32)],
        compiler_params=_params("parallel", "parallel"),
        name="stickbreak_attn",
    )(q, k, v)


def _merge_ffn_body(*refs, fc, n_blocks):
    n_cast = len(n_blocks)
    (x_ref, sb_ref, cv_ref, gl_ref, gates_ref, wb_ref, wo_ref, mpost_ref,
     pre_ref, post_ref, wg_ref, wu_ref, wd_ref) = refs[:13]
    cast_src = refs[13:13 + n_cast]
    o_ref = refs[13 + n_cast]
    cast_dst = refs[14 + n_cast:14 + 2 * n_cast]
    acc_ref = refs[14 + 2 * n_cast]
    _run_casts(pl.program_id(0), cast_src, cast_dst, n_blocks)
    d = x_ref.shape[1]
    w = sb_ref.shape[1]
    merged = gates_ref[:, 0:d].astype(F32) * _dot(sb_ref[...], wb_ref[0:w, :])
    merged = merged + gates_ref[:, d:2 * d].astype(F32) * _dot(cv_ref[...], wb_ref[w:2 * w, :])
    merged = merged + gates_ref[:, 2 * d:3 * d].astype(F32) * _dot(gl_ref[...], wb_ref[2 * w:3 * w, :])
    m = _dot(merged.astype(BF16), wo_ref[...])
    x = x_ref[...] + _rms(m, mpost_ref[...])
    o_ref[...] = _ffn_tile(x, pre_ref[...], post_ref[...], wg_ref, wu_ref, wd_ref, acc_ref, fc)


def _merge_ffn(x, sb, cv, gl, gates, wb, wo, mpost, pre, post, wg, wu, wd, cast_srcs, layer,
               *, tm=512, fc=512):
    t, d = x.shape
    w = sb.shape[1]
    c_in, c_out, c_shapes, n_blocks = _cast_jobs(cast_srcs, layer, lambda i: i, t // tm)

    def row(n):
        return pl.BlockSpec((tm, n), lambda i: (i, 0))

    vec = _resident((1, d))
    return pl.pallas_call(
        functools.partial(_merge_ffn_body, fc=fc, n_blocks=tuple(n_blocks)),
        grid=(t // tm,),
        in_specs=[row(d), row(w), row(w), row(w), row(3 * d), _resident(wb.shape),
                  _resident(wo.shape), vec, vec, vec,
                  _resident(wg.shape), _resident(wu.shape), _resident(wd.shape)] + c_in,
        out_specs=[row(d)] + c_out,
        out_shape=[jax.ShapeDtypeStruct((t, d), F32)] + c_shapes,
        scratch_shapes=[pltpu.VMEM((tm, d), F32)],
        compiler_params=_params("arbitrary"),
        name="merge_ffn",
    )(x, sb, cv, gl, gates, wb, wo, mpost.reshape(1, d), pre.reshape(1, d), post.reshape(1, d),
      wg, wu, wd, *cast_srcs)


def kernel(x, norm_pre, norm_post, ffn1_w_gate, ffn1_w_up, ffn1_w_down, ffn2_w_gate, ffn2_w_up,
           ffn2_w_down, w_in, conv_w, conv_b, conv_ln_g, conv_ln_b, gla_w_alpha, gla_b_alpha,
           gla_norm_g, w_branch, w_out):
    b, s, d = x.shape
    t = b * s
    n_layers = norm_pre.shape[0]
    walpha = jnp.pad(gla_w_alpha, ((0, 0), (0, LANES - GLA_GATE_RANK), (0, 0)))
    w_branch2 = w_branch.reshape(n_layers, N_BRANCHES * BRANCH_WIDTH, d)
    w_in_t = jnp.swapaxes(w_in, 1, 2)
    ffn1_srcs = [ffn1_w_gate, ffn1_w_up, ffn1_w_down]
    mix_srcs = [ffn2_w_gate, ffn2_w_up, ffn2_w_down, w_branch2, w_out]

    def flat(a):
        return a.reshape(t, a.shape[2])

    ffn1 = [a[0].astype(BF16) for a in ffn1_srcs]
    xf = x.reshape(t, d)
    for l in range(n_layers):
        xf, w_t = _ffn(xf, norm_pre[l, 0], norm_post[l, 0], *ffn1, w_in_t, l)
        q, k, v, gates, cv, gl, *mix_w = _mixer_in(
            xf.reshape(b, s, d), norm_pre[l, 1], w_t, walpha[l], gla_b_alpha[l],
            conv_w[l], conv_b[l], conv_ln_g[l], conv_ln_b[l], gla_norm_g[l], mix_srcs, l)
        sb = _attn(q, k, v)
        ffn2, (wb, wo) = mix_w[:3], mix_w[3:]
        nxt = ffn1_srcs if l + 1 < n_layers else []
        xf, *ffn1 = _merge_ffn(xf, flat(sb), flat(cv), flat(gl), flat(gates), wb, wo,
                               norm_post[l, 1], norm_pre[l, 2], norm_post[l, 2], *ffn2, nxt, l + 1)
    return xf.reshape(b, s, d)
```

```python
import functools

import jax
import jax.numpy as jnp
from jax import lax
from jax.experimental import pallas as pl
from jax.experimental.pallas import tpu as pltpu

NORM_EPS = 1e-6
BRANCH_WIDTH = 512
SB_HEAD_DIM = 64
CONV_WIDTH = 31
GLA_KEY_DIM = 256
GLA_HEADS = 4
GLA_HEAD_K = 64
GLA_HEAD_V = 128
GLA_GATE_RANK = 16
GLA_GATE_TAU = 16.0
GLA_CHUNK = 64
N_BRANCHES = 3
FFN_ROWS = 128
GATE_CHUNK = 256

DEAD_STICK = 105.0

LANES = 128
SUBLANES = 8
BF16_ROWS = 16
VMEM_LIMIT_BYTES = 56 * 1024 * 1024

F32 = jnp.float32
BF16 = jnp.bfloat16


def _dot(a, b):
    return jnp.dot(a, b, preferred_element_type=F32)


def _dot_nt(a, b):
    return lax.dot_general(a, b, (((1,), (1,)), ((), ())), preferred_element_type=F32)


def _dot_tn(a, b):
    return lax.dot_general(a, b, (((0,), (0,)), ((), ())), preferred_element_type=F32)


def _rms(x, g):
    return x * lax.rsqrt(jnp.mean(x * x, axis=-1, keepdims=True) + NORM_EPS) * g


def _sigmoid(x):
    return 1.0 / (1.0 + jnp.exp(-x))


def _params(*sem):
    return pltpu.CompilerParams(dimension_semantics=sem, vmem_limit_bytes=VMEM_LIMIT_BYTES)


def _resident(shape):
    nd = len(shape)
    return pl.BlockSpec(shape, lambda *_: (0,) * nd, pipeline_mode=pl.Buffered(1))


def _cast_jobs(srcs, layer, step_of, n_steps):
    in_specs, out_specs, out_shapes, n_blocks = [], [], [], []
    for src in srcs:
        _, r, c = src.shape
        rows = next(m for m in range(BF16_ROWS, r + 1, BF16_ROWS) if r % m == 0 and r // m <= n_steps)
        nb = r // rows
        in_specs.append(pl.BlockSpec(
            (None, rows, c), lambda *g, nb=nb: (layer, jnp.minimum(step_of(*g), nb - 1), 0)))
        out_specs.append(pl.BlockSpec(
            (rows, c), lambda *g, nb=nb: (jnp.minimum(step_of(*g), nb - 1), 0)))
        out_shapes.append(jax.ShapeDtypeStruct((r, c), BF16))
        n_blocks.append(nb)
    return in_specs, out_specs, out_shapes, n_blocks


def _run_casts(step, src_refs, dst_refs, n_blocks):
    for src, dst, nb in zip(src_refs, dst_refs, n_blocks):
        @pl.when(step < nb)
        def _(src=src, dst=dst):
            dst[...] = src[...].astype(BF16)


def _ffn_tile(x, pre, post, wg_ref, wu_ref, wd_ref, acc_ref, fc):
    tm = x.shape[0]
    d_ff = wg_ref.shape[1]
    slabs = [slice(r, r + FFN_ROWS) for r in range(0, tm, FFN_ROWS)]
    chunks = [(c0, min(c0 + fc, d_ff)) for c0 in range(0, d_ff, fc)]

    def hidden(hh, c0, c1):
        g = _dot(hh, wg_ref[:, c0:c1])
        return (g * _sigmoid(g) * _dot(hh, wu_ref[:, c0:c1])).astype(BF16)

    hs = [_rms(x[r], pre).astype(BF16) for r in slabs]
    h = jnp.concatenate(hs, axis=0)
    out = None
    for n, (c0, c1) in enumerate(chunks):
        if n == 0:
            a = jnp.concatenate([hidden(hh, c0, c1) for hh in hs], axis=0)
        else:
            a = hidden(h, c0, c1)
        if n == len(chunks) - 1:
            out = jnp.concatenate(
                [x[r] + 0.5 * _rms(acc_ref[r, :] + _dot(a[r], wd_ref[c0:c1, :]), post) for r in slabs], axis=0)
        elif n == 0:
            acc_ref[...] = _dot(a, wd_ref[c0:c1, :])
        else:
            acc_ref[...] += _dot(a, wd_ref[c0:c1, :])
    return out


def _ffn_body(x_ref, pre_ref, post_ref, wg_ref, wu_ref, wd_ref, wint_ref, o_ref, wt_ref, acc_ref, *, fc):
    o_ref[...] = _ffn_tile(x_ref[...], pre_ref[...], post_ref[...], wg_ref, wu_ref, wd_ref, acc_ref, fc)
    wt_ref[...] = wint_ref[...].astype(BF16)


def _ffn(x, pre, post, wg, wu, wd, w_in_t, layer, *, tm=512, fc=512):
    t, d = x.shape
    n_steps = t // tm
    n_in = w_in_t.shape[1]
    rows = -(-n_in // (n_steps * BF16_ROWS)) * BF16_ROWS
    last = -(-n_in // rows) - 1
    row = pl.BlockSpec((tm, d), lambda i: (i, 0))
    return pl.pallas_call(
        functools.partial(_ffn_body, fc=fc),
        grid=(n_steps,),
        in_specs=[row, _resident((1, d)), _resident((1, d)),
                  _resident(wg.shape), _resident(wu.shape), _resident(wd.shape),
                  pl.BlockSpec((None, rows, d), lambda i: (layer, jnp.minimum(i, last), 0))],
        out_specs=[row, pl.BlockSpec((rows, d), lambda i: (jnp.minimum(i, last), 0))],
        out_shape=[jax.ShapeDtypeStruct((t, d), F32), jax.ShapeDtypeStruct((n_in, d), BF16)],
        scratch_shapes=[pltpu.VMEM((tm, d), F32)],
        compiler_params=_params("arbitrary"),
        name="ffn",
    )(x, pre.reshape(1, d), post.reshape(1, d), wg, wu, wd, w_in_t)


def _conv_tile(ext_ref, shift, w, bias, gain, beta, o_ref, *, tm, rb, halo):
    ne = rb + halo
    first = halo - (CONV_WIDTH - 1)

    def block(r0):
        shifted = _dot(shift, ext_ref[r0:r0 + ne, :])
        acc = bias
        for j in range(CONV_WIDTH):
            b = (first + j) % SUBLANES
            start = b * ne + (first + j - b)
            acc = acc + w[j:j + 1, :] * shifted[start:start + rb]
        mu = jnp.mean(acc, axis=-1, keepdims=True)
        yc = acc - mu
        var = jnp.mean(yc * yc, axis=-1, keepdims=True)
        yn = yc * lax.rsqrt(var + NORM_EPS) * gain + beta
        o_ref[0, r0:r0 + rb, :] = (yn * _sigmoid(yn)).astype(BF16)

    return [functools.partial(block, r0) for r0 in range(0, tm, rb)]


def _gla_tile(q, k, v, r, la, gain, state_ref, o_ref, *, tm):
    ch = GLA_CHUNK
    rr = lax.broadcasted_iota(jnp.int32, (ch, ch), 0)
    cc = lax.broadcasted_iota(jnp.int32, (ch, ch), 1)
    later = jnp.where(cc > rr, 1.0, 0.0).astype(BF16)
    key_head = lax.broadcasted_iota(jnp.int32, (1, GLA_KEY_DIM), 1) // GLA_HEAD_K

    n_chunks = tm // ch
    carry = [state_ref[...]]

    def chunk(n):
        state = carry[0]
        rows = slice(n * ch, (n + 1) * ch)
        la_c = la[rows]
        la_hi = la_c.astype(BF16)
        la_lo = (la_c - la_hi.astype(F32)).astype(BF16)
        decay_to_end = _dot(later, la_hi) + _dot(later, la_lo)
        chunk_decay = jnp.exp(decay_to_end[0:1, :] + la_c[0:1, :])
        k_dec = (k[rows].astype(F32) * jnp.exp(decay_to_end)).astype(BF16)
        state = state * chunk_decay + _dot_tn(v[rows], k_dec)
        st = state.astype(BF16)
        q_c = q[rows]
        normed = []
        for h in range(GLA_HEADS):
            qh = jnp.where(key_head == h, q_c, jnp.zeros_like(q_c))
            oh = _dot_nt(qh, st[h * GLA_HEAD_V:(h + 1) * GLA_HEAD_V, :])
            normed.append(oh * lax.rsqrt(jnp.mean(oh * oh, axis=-1, keepdims=True) + NORM_EPS))
        rg = r[rows].astype(F32)
        o_ref[0, rows, :] = (jnp.concatenate(normed, axis=1) * gain * (rg * _sigmoid(rg))).astype(BF16)
        carry[0] = state
        if n == n_chunks - 1:
            state_ref[...] = state

    return [functools.partial(chunk, n) for n in range(n_chunks)]


def _mixer_in_body(*refs, tm, rb, halo, n_blocks):
    n_cast = len(n_blocks)
    (x_ref, pre_ref, wt_ref, walpha_ref, balpha_ref,
     shift_ref, cw_ref, cb_ref, cg_ref, cbeta_ref, gg_ref) = refs[:11]
    cast_src = refs[11:11 + n_cast]
    q_ref, k_ref, v_ref, gates_ref, cv_ref, gl_ref = refs[11 + n_cast:17 + n_cast]
    cast_dst = refs[17 + n_cast:17 + 2 * n_cast]
    ext_ref, state_ref = refs[17 + 2 * n_cast:]
    _run_casts(pl.program_id(0) * pl.num_programs(1) + pl.program_id(1), cast_src, cast_dst, n_blocks)
    w = BRANCH_WIDTH
    dk = GLA_KEY_DIM

    @pl.when(pl.program_id(1) == 0)
    def _():
        ext_ref[0:halo, :] = jnp.zeros((halo, w), BF16)
        state_ref[...] = jnp.zeros_like(state_ref)

    @pl.when(pl.program_id(1) > 0)
    def _():
        ext_ref[0:halo, :] = ext_ref[tm:tm + halo, :]

    h = _rms(x_ref[0], pre_ref[...]).astype(BF16)
    o_conv = 3 * w
    o_gla = o_conv + 2 * w
    o_lr = o_gla + 2 * dk + 2 * w
    o_gate = o_lr + GLA_GATE_RANK

    c = _dot_nt(h, wt_ref[o_conv:o_gla, :])
    ext_ref[halo:halo + tm, :] = (c[:, :w] * _sigmoid(c[:, w:])).astype(BF16)
    conv_tasks = _conv_tile(ext_ref, shift_ref[...], cw_ref[...], cb_ref[...], cg_ref[...], cbeta_ref[...],
                            cv_ref, tm=tm, rb=rb, halo=halo)

    n_gate = gates_ref.shape[2]
    gate_cols = [(c0, min(c0 + GATE_CHUNK, n_gate)) for c0 in range(0, n_gate, GATE_CHUNK)]

    def gate_chunk(c0, c1):
        gates_ref[0, :, c0:c1] = _sigmoid(_dot_nt(h, wt_ref[o_gate + c0:o_gate + c1, :])).astype(BF16)

    n_gla = tm // GLA_CHUNK
    early = gate_cols[n_gla:]
    for j, (ref, scale) in enumerate(((q_ref, SB_HEAD_DIM ** -0.5), (k_ref, 1.0), (v_ref, 1.0))):
        ref[0] = (_dot_nt(h, wt_ref[j * w:(j + 1) * w, :]) * scale).astype(BF16)
        if j < len(early):
            gate_chunk(*early[j])
        if j < len(conv_tasks):
            conv_tasks[j]()
    g4 = []
    for j in range(3):
        g4.append(_dot_nt(h, wt_ref[o_gla + j * w:o_gla + (j + 1) * w, :]))
        if j == 0:
            for cols in early[3:]:
                gate_chunk(*cols)
            for task in conv_tasks[3:]:
                task()
    g4 = jnp.concatenate(g4, axis=1)
    lane = lax.broadcasted_iota(jnp.int32, (1, LANES), 1)
    lr = jnp.where(lane < GLA_GATE_RANK, _dot_nt(h, wt_ref[o_lr:o_lr + LANES, :]), 0.0)
    a = jnp.dot(lr, walpha_ref[...], preferred_element_type=F32,
                precision=lax.Precision.HIGHEST) + balpha_ref[...]
    la = (jnp.minimum(a, 0.0) - jnp.log(1.0 + jnp.exp(-jnp.abs(a)))) * (1.0 / GLA_GATE_TAU)
    tasks = _gla_tile((g4[:, :dk] * (GLA_HEAD_K ** -0.5)).astype(BF16), g4[:, dk:2 * dk].astype(BF16),
                      g4[:, 2 * dk:2 * dk + w].astype(BF16), g4[:, 2 * dk + w:].astype(BF16), la,
                      gg_ref[...], state_ref, gl_ref, tm=tm)
    assert len(tasks) == n_gla <= len(gate_cols)
    for cols, task in zip(gate_cols[:n_gla], tasks):
        gate_chunk(*cols)
        task()


def _mixer_in(x, pre, w_t, walpha, balpha, conv_w, conv_b, conv_g,
              conv_beta, gla_g, cast_srcs, layer, *, tm=512, rb=128, halo=32):
    b, s, d = x.shape
    w = BRANCH_WIDTH
    dk = GLA_KEY_DIM
    n_gate = N_BRANCHES * d
    assert w_t.shape[0] == 5 * w + 2 * dk + 2 * w + GLA_GATE_RANK + n_gate
    n_seq = s // tm
    c_in, c_out, c_shapes, n_blocks = _cast_jobs(cast_srcs, layer, lambda bi, i: bi * n_seq + i, b * n_seq)
    assert halo >= CONV_WIDTH - 1 and halo % (2 * SUBLANES) == 0 and tm % rb == 0 and s % tm == 0
    ne = rb + halo
    m = jnp.arange(ne)
    shift = jnp.concatenate([(m[None, :] == m[:, None] + j) for j in range(SUBLANES)], axis=0).astype(BF16)

    def row(n):
        return pl.BlockSpec((1, tm, n), lambda bi, i: (bi, i, 0))

    def out(n):
        return jax.ShapeDtypeStruct((b, s, n), BF16)

    vec = _resident((1, w))
    return pl.pallas_call(
        functools.partial(_mixer_in_body, tm=tm, rb=rb, halo=halo, n_blocks=tuple(n_blocks)),
        grid=(b, n_seq),
        in_specs=[row(d), _resident((1, d)), _resident(w_t.shape),
                  _resident(walpha.shape), _resident((1, dk)), _resident(shift.shape),
                  _resident(conv_w.shape), vec, vec, vec, vec] + c_in,
        out_specs=[row(w), row(w), row(w), row(n_gate), row(w), row(w)] + c_out,
        out_shape=[out(w), out(w), out(w), out(n_gate), out(w), out(w)] + c_shapes,
        scratch_shapes=[pltpu.VMEM((halo + tm, w), BF16), pltpu.VMEM((w, dk), F32)],
        compiler_params=_params("arbitrary", "arbitrary"),
        name="mixer_in",
    )(x, pre.reshape(1, d), w_t, walpha, balpha.reshape(1, dk), shift,
      conv_w, conv_b.reshape(1, w), conv_g.reshape(1, w), conv_beta.reshape(1, w), gla_g.reshape(1, w),
      *cast_srcs)


def _attn_body(q_ref, k_ref, v_ref, o_ref, stick_ref, acc_ref, *, tb, nq, window, band):
    i = pl.program_id(1)
    n_pairs = q_ref.shape[2] // LANES
    hb = tb // 2
    wk = window * tb
    lane = lax.broadcasted_iota(jnp.int32, (1, LANES), 1)
    head0 = lane < SB_HEAD_DIM

    def pair(ref, r0, n, p):
        return ref[0, pl.ds(r0, n), p * LANES:(p + 1) * LANES]

    def stacked_q(p, r0, n):
        q = pair(q_ref, r0, n, p)
        zero = jnp.zeros_like(q)
        return jnp.concatenate([jnp.where(head0, q, zero), jnp.where(head0, zero, q)], axis=0)

    def log_terms(z):
        lg = jnp.log(1.0 + jnp.exp(-jnp.abs(z)))
        sp = jnp.maximum(z, 0.0) + lg
        return sp, z - sp

    def write_out(acc, r0, n, p):
        o_ref[0, pl.ds(r0, n), p * LANES:(p + 1) * LANES] = jnp.where(head0, acc[:n], acc[n:]).astype(BF16)

    def tri(n):
        return jnp.where(lax.broadcasted_iota(jnp.int32, (n, n), 0) > lax.broadcasted_iota(jnp.int32, (n, n), 1),
                         1.0, 0.0)

    def query_row(n_rows, n_cols):
        r = lax.broadcasted_iota(jnp.int32, (2 * n_rows, n_cols), 0)
        return jnp.where(r >= n_rows, r - n_rows, r)

    def masked_pass(qs_list, w0_list, width, mask):
        sps, lss = [], []
        for (p, qs), w0 in zip(qs_list, w0_list):
            sp, ls = log_terms(_dot_nt(qs, pair(k_ref, w0, width, p)))
            sps.append(mask(sp))
            lss.append(ls)
        later = _dot(jnp.concatenate(sps, axis=0).astype(BF16), tri(width).astype(BF16))
        n = later.shape[0] // len(sps)
        out = []
        for e, ((p, _), w0) in enumerate(zip(qs_list, w0_list)):
            lt = later[e * n:(e + 1) * n]
            wgt = mask(jnp.exp(lss[e] - lt))
            out.append((_dot(wgt.astype(BF16), pair(v_ref, w0, width, p)), lt[:, 0:1] + sps[e][:, 0:1]))
        return out

    def last_tile_mask(keep, width):
        def mask(a):
            return jnp.concatenate([a[:, :width - LANES], jnp.where(keep, a[:, width - LANES:], 0.0)], axis=1)
        return mask

    def band_pass():
        keep = lax.broadcasted_iota(jnp.int32, (2 * hb, LANES), 1) < query_row(hb, LANES) + (LANES - hb)
        entries, starts, rows0 = [], [], []
        for half in range(2 * nq):
            r0 = half * hb
            w0 = pl.multiple_of(i * (nq * tb) + (r0 + hb - band), hb)
            for p in range(n_pairs):
                entries.append((p, stacked_q(p, r0, hb)))
                starts.append(w0)
                rows0.append(r0)
        res = masked_pass(entries, starts, band, last_tile_mask(keep, band))
        mins = [None] * nq
        for (acc, stick), (p, _), r0 in zip(res, entries, rows0):
            write_out(acc, r0, hb, p)
            blk = r0 // tb
            mins[blk] = stick if mins[blk] is None else jnp.minimum(mins[blk], stick)
        return [jnp.min(m) for m in mins]

    def window_pass(blk_r0, w0, keep, aligned):
        if aligned:
            mask = last_tile_mask(keep, wk)
        else:
            def mask(a):
                return jnp.where(keep, a, 0.0)
        entries = [(p, stacked_q(p, blk_r0, tb)) for p in range(n_pairs)]
        res = masked_pass(entries, [w0] * n_pairs, wk, mask)
        for (acc, _), p in zip(res, range(n_pairs)):
            write_out(acc, blk_r0, tb, p)
        return res

    def block_step(j0, p, blk_r0, stick, acc, suffix):
        sp, ls = log_terms(_dot_nt(stacked_q(p, blk_r0, tb), pair(k_ref, j0, tb, p)))
        sums = _dot(sp.astype(BF16), suffix)
        wgt = jnp.exp(ls - sums[:, :tb] - stick)
        return stick + sums[:, tb:], acc + _dot(wgt.astype(BF16), pair(v_ref, j0, tb, p))

    def min_stick(sticks):
        m = sticks[0]
        for s in sticks[1:]:
            m = jnp.minimum(m, s)
        return jnp.min(m)

    def exact_block(blk):
        blk_r0 = blk * tb
        ib = i * nq + blk
        keep = lax.broadcasted_iota(jnp.int32, (2 * tb, LANES), 1) < query_row(tb, LANES) + (LANES - tb)
        res = window_pass(blk_r0, pl.multiple_of((ib - (window - 1)) * tb, tb), keep, True)
        for p in range(n_pairs):
            acc_ref[p] = res[p][0]
            stick_ref[p] = jnp.broadcast_to(res[p][1], (2 * tb, tb))
        suffix = jnp.concatenate([tri(tb), jnp.ones((tb, tb), F32)], axis=1).astype(BF16)

        def cond(st):
            return jnp.logical_and(st[0] >= 0, st[1] > 0)

        def body(st):
            j0 = pl.multiple_of(st[0] * tb, tb)
            new = []
            for p in range(n_pairs):
                stick, acc = block_step(j0, p, blk_r0, stick_ref[p], acc_ref[p], suffix)
                stick_ref[p] = stick
                acc_ref[p] = acc
                new.append(stick)
            return st[0] - 1, (min_stick(new) < DEAD_STICK).astype(jnp.int32)

        alive = (min_stick([r[1] for r in res]) < DEAD_STICK).astype(jnp.int32)
        lax.while_loop(cond, body, (ib - window, alive))
        for p in range(n_pairs):
            write_out(acc_ref[p], blk_r0, tb, p)

    @pl.when(i > 0)
    def _():
        for blk, lowest in enumerate(band_pass()):
            pl.when(lowest < DEAD_STICK)(functools.partial(exact_block, blk))

    @pl.when(i == 0)
    def _():
        for blk in range(nq):
            keep = lax.broadcasted_iota(jnp.int32, (2 * tb, wk), 1) < query_row(tb, wk) + blk * tb
            window_pass(blk * tb, 0, keep, False)


def _attn(q, k, v, *, tb=128, nq=4, window=4, band=256):
    b, s, w = q.shape
    tq = nq * tb
    assert s % tq == 0 and w % LANES == 0 and tq <= window * tb <= s
    assert band % LANES == 0 and tb // 2 <= LANES and band - tb // 2 <= tq and (window - 1) * tb <= tq
    n_pairs = w // LANES
    qspec = pl.BlockSpec((1, tq, w), lambda bi, i: (bi, i, 0))
    kvspec = pl.BlockSpec((1, s, w), lambda bi, i: (bi, 0, 0))
    return pl.pallas_call(
        functools.partial(_attn_body, tb=tb, nq=nq, window=window, band=band),
        grid=(b, s // tq),
        in_specs=[qspec, kvspec, kvspec],
        out_specs=qspec,
        out_shape=jax.ShapeDtypeStruct((b, s, w), BF16),
        scratch_shapes=[pltpu.VMEM((n_pairs, 2 * tb, tb), F32), pltpu.VMEM((n_pairs, 2 * tb, LANES), F32)],
        compiler_params=_params("parallel", "parallel"),
        name="stickbreak_attn",
    )(q, k, v)


def _merge_ffn_body(*refs, fc, n_blocks):
    n_cast = len(n_blocks)
    (x_ref, sb_ref, cv_ref, gl_ref, gates_ref, wb_ref, wo_ref, mpost_ref,
     pre_ref, post_ref, wg_ref, wu_ref, wd_ref) = refs[:13]
    cast_src = refs[13:13 + n_cast]
    o_ref = refs[13 + n_cast]
    cast_dst = refs[14 + n_cast:14 + 2 * n_cast]
    acc_ref = refs[14 + 2 * n_cast]
    _run_casts(pl.program_id(0), cast_src, cast_dst, n_blocks)
    d = x_ref.shape[1]
    w = sb_ref.shape[1]
    merged = gates_ref[:, 0:d].astype(F32) * _dot(sb_ref[...], wb_ref[0:w, :])
    merged = merged + gates_ref[:, d:2 * d].astype(F32) * _dot(cv_ref[...], wb_ref[w:2 * w, :])
    merged = merged + gates_ref[:, 2 * d:3 * d].astype(F32) * _dot(gl_ref[...], wb_ref[2 * w:3 * w, :])
    m = _dot(merged.astype(BF16), wo_ref[...])
    x = x_ref[...] + _rms(m, mpost_ref[...])
    o_ref[...] = _ffn_tile(x, pre_ref[...], post_ref[...], wg_ref, wu_ref, wd_ref, acc_ref, fc)


def _merge_ffn(x, sb, cv, gl, gates, wb, wo, mpost, pre, post, wg, wu, wd, cast_srcs, layer,
               *, tm=512, fc=512):
    t, d = x.shape
    w = sb.shape[1]
    c_in, c_out, c_shapes, n_blocks = _cast_jobs(cast_srcs, layer, lambda i: i, t // tm)

    def row(n):
        return pl.BlockSpec((tm, n), lambda i: (i, 0))

    vec = _resident((1, d))
    return pl.pallas_call(
        functools.partial(_merge_ffn_body, fc=fc, n_blocks=tuple(n_blocks)),
        grid=(t // tm,),
        in_specs=[row(d), row(w), row(w), row(w), row(3 * d), _resident(wb.shape),
                  _resident(wo.shape), vec, vec, vec,
                  _resident(wg.shape), _resident(wu.shape), _resident(wd.shape)] + c_in,
        out_specs=[row(d)] + c_out,
        out_shape=[jax.ShapeDtypeStruct((t, d), F32)] + c_shapes,
        scratch_shapes=[pltpu.VMEM((tm, d), F32)],
        compiler_params=_params("arbitrary"),
        name="merge_ffn",
    )(x, sb, cv, gl, gates, wb, wo, mpost.reshape(1, d), pre.reshape(1, d), post.reshape(1, d),
      wg, wu, wd, *cast_srcs)


def kernel(x, norm_pre, norm_post, ffn1_w_gate, ffn1_w_up, ffn1_w_down, ffn2_w_gate, ffn2_w_up,
           ffn2_w_down, w_in, conv_w, conv_b, conv_ln_g, conv_ln_b, gla_w_alpha, gla_b_alpha,
           gla_norm_g, w_branch, w_out):
    b, s, d = x.shape
    t = b * s
    n_layers = norm_pre.shape[0]
    walpha = jnp.pad(gla_w_alpha, ((0, 0), (0, LANES - GLA_GATE_RANK), (0, 0)))
    w_branch2 = w_branch.reshape(n_layers, N_BRANCHES * BRANCH_WIDTH, d)
    w_in_t = jnp.swapaxes(w_in, 1, 2)
    ffn1_srcs = [ffn1_w_gate, ffn1_w_up, ffn1_w_down]
    mix_srcs = [ffn2_w_gate, ffn2_w_up, ffn2_w_down, w_branch2, w_out]

    def flat(a):
        return a.reshape(t, a.shape[2])

    ffn1 = [a[0].astype(BF16) for a in ffn1_srcs]
    xf = x.reshape(t, d)
    for l in range(n_layers):
        xf, w_t = _ffn(xf, norm_pre[l, 0], norm_post[l, 0], *ffn1, w_in_t, l)
        q, k, v, gates, cv, gl, *mix_w = _mixer_in(
            xf.reshape(b, s, d), norm_pre[l, 1], w_t, walpha[l], gla_b_alpha[l],
            conv_w[l], conv_b[l], conv_ln_g[l], conv_ln_b[l], gla_norm_g[l], mix_srcs, l)
        sb = _attn(q, k, v)
        ffn2, (wb, wo) = mix_w[:3], mix_w[3:]
        nxt = ffn1_srcs if l + 1 < n_layers else []
        xf, *ffn1 = _merge_ffn(xf, flat(sb), flat(cv), flat(gl), flat(gates), wb, wo,
                               norm_post[l, 1], norm_pre[l, 2], norm_post[l, 2], *ffn2, nxt, l + 1)
    return xf.reshape(b, s, d)
```

```python
import functools

import jax
import jax.numpy as jnp
from jax import lax
from jax.experimental import pallas as pl
from jax.experimental.pallas import tpu as pltpu

NORM_EPS = 1e-6
BRANCH_WIDTH = 512
SB_HEAD_DIM = 64
CONV_WIDTH = 31
GLA_KEY_DIM = 256
GLA_HEADS = 4
GLA_HEAD_K = 64
GLA_HEAD_V = 128
GLA_GATE_RANK = 16
GLA_GATE_TAU = 16.0
GLA_CHUNK = 64
N_BRANCHES = 3
FFN_ROWS = 128
GATE_CHUNK = 256

DEAD_STICK = 105.0

LANES = 128
SUBLANES = 8
BF16_ROWS = 16
VMEM_LIMIT_BYTES = 56 * 1024 * 1024

F32 = jnp.float32
BF16 = jnp.bfloat16


def _dot(a, b):
    return jnp.dot(a, b, preferred_element_type=F32)


def _dot_nt(a, b):
    return lax.dot_general(a, b, (((1,), (1,)), ((), ())), preferred_element_type=F32)


def _dot_tn(a, b):
    return lax.dot_general(a, b, (((0,), (0,)), ((), ())), preferred_element_type=F32)


def _rms(x, g):
    return x * lax.rsqrt(jnp.mean(x * x, axis=-1, keepdims=True) + NORM_EPS) * g


def _sigmoid(x):
    return 1.0 / (1.0 + jnp.exp(-x))


def _params(*sem):
    return pltpu.CompilerParams(dimension_semantics=sem, vmem_limit_bytes=VMEM_LIMIT_BYTES)


def _resident(shape):
    nd = len(shape)
    return pl.BlockSpec(shape, lambda *_: (0,) * nd, pipeline_mode=pl.Buffered(1))


def _cast_jobs(srcs, layer, step_of, n_steps):
    in_specs, out_specs, out_shapes, n_blocks = [], [], [], []
    for src in srcs:
        _, r, c = src.shape
        rows = next(m for m in range(BF16_ROWS, r + 1, BF16_ROWS) if r % m == 0 and r // m <= n_steps)
        nb = r // rows
        in_specs.append(pl.BlockSpec(
            (None, rows, c), lambda *g, nb=nb: (layer, jnp.minimum(step_of(*g), nb - 1), 0)))
        out_specs.append(pl.BlockSpec(
            (rows, c), lambda *g, nb=nb: (jnp.minimum(step_of(*g), nb - 1), 0)))
        out_shapes.append(jax.ShapeDtypeStruct((r, c), BF16))
        n_blocks.append(nb)
    return in_specs, out_specs, out_shapes, n_blocks


def _run_casts(step, src_refs, dst_refs, n_blocks):
    for src, dst, nb in zip(src_refs, dst_refs, n_blocks):
        @pl.when(step < nb)
        def _(src=src, dst=dst):
            dst[...] = src[...].astype(BF16)


def _ffn_tile(x, pre, post, wg_ref, wu_ref, wd_ref, acc_ref, fc):
    tm = x.shape[0]
    d_ff = wg_ref.shape[1]
    slabs = [slice(r, r + FFN_ROWS) for r in range(0, tm, FFN_ROWS)]
    chunks = [(c0, min(c0 + fc, d_ff)) for c0 in range(0, d_ff, fc)]

    def hidden(hh, c0, c1):
        g = _dot(hh, wg_ref[:, c0:c1])
        return (g * _sigmoid(g) * _dot(hh, wu_ref[:, c0:c1])).astype(BF16)

    hs = [_rms(x[r], pre).astype(BF16) for r in slabs]
    h = jnp.concatenate(hs, axis=0)
    out = None
    for n, (c0, c1) in enumerate(chunks):
        if n == 0:
            a = jnp.concatenate([hidden(hh, c0, c1) for hh in hs], axis=0)
        else:
            a = hidden(h, c0, c1)
        if n == len(chunks) - 1:
            out = jnp.concatenate(
                [x[r] + 0.5 * _rms(acc_ref[r, :] + _dot(a[r], wd_ref[c0:c1, :]), post) for r in slabs], axis=0)
        elif n == 0:
            acc_ref[...] = _dot(a, wd_ref[c0:c1, :])
        else:
            acc_ref[...] += _dot(a, wd_ref[c0:c1, :])
    return out


def _ffn_body(x_ref, pre_ref, post_ref, wg_ref, wu_ref, wd_ref, wint_ref, o_ref, wt_ref, acc_ref, *, fc):
    o_ref[...] = _ffn_tile(x_ref[...], pre_ref[...], post_ref[...], wg_ref, wu_ref, wd_ref, acc_ref, fc)
    wt_ref[...] = wint_ref[...].astype(BF16)


def _ffn(x, pre, post, wg, wu, wd, w_in_t, layer, *, tm=1024, fc=512):
    t, d = x.shape
    n_steps = t // tm
    n_in = w_in_t.shape[1]
    rows = -(-n_in // (n_steps * BF16_ROWS)) * BF16_ROWS
    last = -(-n_in // rows) - 1
    row = pl.BlockSpec((tm, d), lambda i: (i, 0))
    return pl.pallas_call(
        functools.partial(_ffn_body, fc=fc),
        grid=(n_steps,),
        in_specs=[row, _resident((1, d)), _resident((1, d)),
                  _resident(wg.shape), _resident(wu.shape), _resident(wd.shape),
                  pl.BlockSpec((None, rows, d), lambda i: (layer, jnp.minimum(i, last), 0))],
        out_specs=[row, pl.BlockSpec((rows, d), lambda i: (jnp.minimum(i, last), 0))],
        out_shape=[jax.ShapeDtypeStruct((t, d), F32), jax.ShapeDtypeStruct((n_in, d), BF16)],
        scratch_shapes=[pltpu.VMEM((tm, d), F32)],
        compiler_params=_params("arbitrary"),
        name="ffn",
    )(x, pre.reshape(1, d), post.reshape(1, d), wg, wu, wd, w_in_t)


def _conv_tile(ext_ref, shift, w, bias, gain, beta, o_ref, *, tm, rb, halo):
    ne = rb + halo
    first = halo - (CONV_WIDTH - 1)

    def block(r0):
        shifted = _dot(shift, ext_ref[r0:r0 + ne, :])
        acc = bias
        for j in range(CONV_WIDTH):
            b = (first + j) % SUBLANES
            start = b * ne + (first + j - b)
            acc = acc + w[j:j + 1, :] * shifted[start:start + rb]
        mu = jnp.mean(acc, axis=-1, keepdims=True)
        yc = acc - mu
        var = jnp.mean(yc * yc, axis=-1, keepdims=True)
        yn = yc * lax.rsqrt(var + NORM_EPS) * gain + beta
        o_ref[0, r0:r0 + rb, :] = (yn * _sigmoid(yn)).astype(BF16)

    return [functools.partial(block, r0) for r0 in range(0, tm, rb)]


def _gla_tile(q, k, v, r, la, gain, state_ref, o_ref, *, tm):
    ch = GLA_CHUNK
    rr = lax.broadcasted_iota(jnp.int32, (ch, ch), 0)
    cc = lax.broadcasted_iota(jnp.int32, (ch, ch), 1)
    later = jnp.where(cc > rr, 1.0, 0.0).astype(BF16)
    key_head = lax.broadcasted_iota(jnp.int32, (1, GLA_KEY_DIM), 1) // GLA_HEAD_K

    n_chunks = tm // ch
    carry = [state_ref[...]]

    def chunk(n):
        state = carry[0]
        rows = slice(n * ch, (n + 1) * ch)
        la_c = la[rows]
        la_hi = la_c.astype(BF16)
        la_lo = (la_c - la_hi.astype(F32)).astype(BF16)
        decay_to_end = _dot(later, la_hi) + _dot(later, la_lo)
        chunk_decay = jnp.exp(decay_to_end[0:1, :] + la_c[0:1, :])
        k_dec = (k[rows].astype(F32) * jnp.exp(decay_to_end)).astype(BF16)
        state = state * chunk_decay + _dot_tn(v[rows], k_dec)
        st = state.astype(BF16)
        q_c = q[rows]
        normed = []
        for h in range(GLA_HEADS):
            qh = jnp.where(key_head == h, q_c, jnp.zeros_like(q_c))
            oh = _dot_nt(qh, st[h * GLA_HEAD_V:(h + 1) * GLA_HEAD_V, :])
            normed.append(oh * lax.rsqrt(jnp.mean(oh * oh, axis=-1, keepdims=True) + NORM_EPS))
        rg = r[rows].astype(F32)
        o_ref[0, rows, :] = (jnp.concatenate(normed, axis=1) * gain * (rg * _sigmoid(rg))).astype(BF16)
        carry[0] = state
        if n == n_chunks - 1:
            state_ref[...] = state

    return [functools.partial(chunk, n) for n in range(n_chunks)]


def _mixer_in_body(*refs, tm, rb, halo, n_blocks):
    n_cast = len(n_blocks)
    (x_ref, pre_ref, wt_ref, walpha_ref, balpha_ref,
     shift_ref, cw_ref, cb_ref, cg_ref, cbeta_ref, gg_ref) = refs[:11]
    cast_src = refs[11:11 + n_cast]
    q_ref, k_ref, v_ref, gates_ref, cv_ref, gl_ref = refs[11 + n_cast:17 + n_cast]
    cast_dst = refs[17 + n_cast:17 + 2 * n_cast]
    ext_ref, state_ref = refs[17 + 2 * n_cast:]
    _run_casts(pl.program_id(0) * pl.num_programs(1) + pl.program_id(1), cast_src, cast_dst, n_blocks)
    w = BRANCH_WIDTH
    dk = GLA_KEY_DIM

    @pl.when(pl.program_id(1) == 0)
    def _():
        ext_ref[0:halo, :] = jnp.zeros((halo, w), BF16)
        state_ref[...] = jnp.zeros_like(state_ref)

    @pl.when(pl.program_id(1) > 0)
    def _():
        ext_ref[0:halo, :] = ext_ref[tm:tm + halo, :]

    h = _rms(x_ref[0], pre_ref[...]).astype(BF16)
    o_conv = 3 * w
    o_gla = o_conv + 2 * w
    o_lr = o_gla + 2 * dk + 2 * w
    o_gate = o_lr + GLA_GATE_RANK

    qkv = _dot_nt(h, wt_ref[:o_conv, :])
    q_ref[0] = (qkv[:, :w] * (SB_HEAD_DIM ** -0.5)).astype(BF16)
    k_ref[0] = qkv[:, w:2 * w].astype(BF16)
    v_ref[0] = qkv[:, 2 * w:].astype(BF16)

    c = _dot_nt(h, wt_ref[o_conv:o_gla, :])
    ext_ref[halo:halo + tm, :] = (c[:, :w] * _sigmoid(c[:, w:])).astype(BF16)
    tasks = _conv_tile(ext_ref, shift_ref[...], cw_ref[...], cb_ref[...], cg_ref[...], cbeta_ref[...], cv_ref,
                       tm=tm, rb=rb, halo=halo)

    g4 = _dot_nt(h, wt_ref[o_gla:o_lr, :])
    lane = lax.broadcasted_iota(jnp.int32, (1, LANES), 1)
    lr = jnp.where(lane < GLA_GATE_RANK, _dot_nt(h, wt_ref[o_lr:o_lr + LANES, :]), 0.0)
    a = jnp.dot(lr, walpha_ref[...], preferred_element_type=F32,
                precision=lax.Precision.HIGHEST) + balpha_ref[...]
    la = (jnp.minimum(a, 0.0) - jnp.log(1.0 + jnp.exp(-jnp.abs(a)))) * (1.0 / GLA_GATE_TAU)
    tasks += _gla_tile((g4[:, :dk] * (GLA_HEAD_K ** -0.5)).astype(BF16), g4[:, dk:2 * dk].astype(BF16),
                       g4[:, 2 * dk:2 * dk + w].astype(BF16), g4[:, 2 * dk + w:].astype(BF16), la,
                       gg_ref[...], state_ref, gl_ref, tm=tm)

    n_gate = gates_ref.shape[2]
    gate_cols = [(c0, min(c0 + GATE_CHUNK, n_gate)) for c0 in range(0, n_gate, GATE_CHUNK)]
    per = -(-len(tasks) // len(gate_cols))
    for j, (c0, c1) in enumerate(gate_cols):
        gates_ref[0, :, c0:c1] = _sigmoid(_dot_nt(h, wt_ref[o_gate + c0:o_gate + c1, :])).astype(BF16)
        for task in tasks[j * per:(j + 1) * per]:
            task()


def _mixer_in(x, pre, w_t, walpha, balpha, conv_w, conv_b, conv_g,
              conv_beta, gla_g, cast_srcs, layer, *, tm=512, rb=128, halo=32):
    b, s, d = x.shape
    w = BRANCH_WIDTH
    dk = GLA_KEY_DIM
    n_gate = N_BRANCHES * d
    assert w_t.shape[0] == 5 * w + 2 * dk + 2 * w + GLA_GATE_RANK + n_gate
    n_seq = s // tm
    c_in, c_out, c_shapes, n_blocks = _cast_jobs(cast_srcs, layer, lambda bi, i: bi * n_seq + i, b * n_seq)
    assert halo >= CONV_WIDTH - 1 and halo % (2 * SUBLANES) == 0 and tm % rb == 0 and s % tm == 0
    ne = rb + halo
    m = jnp.arange(ne)
    shift = jnp.concatenate([(m[None, :] == m[:, None] + j) for j in range(SUBLANES)], axis=0).astype(BF16)

    def row(n):
        return pl.BlockSpec((1, tm, n), lambda bi, i: (bi, i, 0))

    def out(n):
        return jax.ShapeDtypeStruct((b, s, n), BF16)

    vec = _resident((1, w))
    return pl.pallas_call(
        functools.partial(_mixer_in_body, tm=tm, rb=rb, halo=halo, n_blocks=tuple(n_blocks)),
        grid=(b, n_seq),
        in_specs=[row(d), _resident((1, d)), _resident(w_t.shape),
                  _resident(walpha.shape), _resident((1, dk)), _resident(shift.shape),
                  _resident(conv_w.shape), vec, vec, vec, vec] + c_in,
        out_specs=[row(w), row(w), row(w), row(n_gate), row(w), row(w)] + c_out,
        out_shape=[out(w), out(w), out(w), out(n_gate), out(w), out(w)] + c_shapes,
        scratch_shapes=[pltpu.VMEM((halo + tm, w), BF16), pltpu.VMEM((w, dk), F32)],
        compiler_params=_params("arbitrary", "arbitrary"),
        name="mixer_in",
    )(x, pre.reshape(1, d), w_t, walpha, balpha.reshape(1, dk), shift,
      conv_w, conv_b.reshape(1, w), conv_g.reshape(1, w), conv_beta.reshape(1, w), gla_g.reshape(1, w),
      *cast_srcs)


def _attn_body(q_ref, k_ref, v_ref, o_ref, stick_ref, acc_ref, *, tb, nq, window, band):
    i = pl.program_id(1)
    n_pairs = q_ref.shape[2] // LANES
    hb = tb // 2
    wk = window * tb
    lane = lax.broadcasted_iota(jnp.int32, (1, LANES), 1)
    head0 = lane < SB_HEAD_DIM

    def pair(ref, r0, n, p):
        return ref[0, pl.ds(r0, n), p * LANES:(p + 1) * LANES]

    def stacked_q(p, r0, n):
        q = pair(q_ref, r0, n, p)
        zero = jnp.zeros_like(q)
        return jnp.concatenate([jnp.where(head0, q, zero), jnp.where(head0, zero, q)], axis=0)

    def log_terms(z):
        lg = jnp.log(1.0 + jnp.exp(-jnp.abs(z)))
        sp = jnp.maximum(z, 0.0) + lg
        return sp, z - sp

    def write_out(acc, r0, n, p):
        o_ref[0, pl.ds(r0, n), p * LANES:(p + 1) * LANES] = jnp.where(head0, acc[:n], acc[n:]).astype(BF16)

    def tri(n):
        return jnp.where(lax.broadcasted_iota(jnp.int32, (n, n), 0) > lax.broadcasted_iota(jnp.int32, (n, n), 1),
                         1.0, 0.0)

    def query_row(n_rows, n_cols):
        r = lax.broadcasted_iota(jnp.int32, (2 * n_rows, n_cols), 0)
        return jnp.where(r >= n_rows, r - n_rows, r)

    def masked_pass(qs_list, w0_list, width, mask):
        sps, lss = [], []
        for (p, qs), w0 in zip(qs_list, w0_list):
            sp, ls = log_terms(_dot_nt(qs, pair(k_ref, w0, width, p)))
            sps.append(mask(sp))
            lss.append(ls)
        later = _dot(jnp.concatenate(sps, axis=0).astype(BF16), tri(width).astype(BF16))
        n = later.shape[0] // len(sps)
        out = []
        for e, ((p, _), w0) in enumerate(zip(qs_list, w0_list)):
            lt = later[e * n:(e + 1) * n]
            wgt = mask(jnp.exp(lss[e] - lt))
            out.append((_dot(wgt.astype(BF16), pair(v_ref, w0, width, p)), lt[:, 0:1] + sps[e][:, 0:1]))
        return out

    def last_tile_mask(keep, width):
        def mask(a):
            return jnp.concatenate([a[:, :width - LANES], jnp.where(keep, a[:, width - LANES:], 0.0)], axis=1)
        return mask

    def band_pass():
        keep = lax.broadcasted_iota(jnp.int32, (2 * hb, LANES), 1) < query_row(hb, LANES) + (LANES - hb)
        entries, starts, rows0 = [], [], []
        for half in range(2 * nq):
            r0 = half * hb
            w0 = pl.multiple_of(i * (nq * tb) + (r0 + hb - band), hb)
            for p in range(n_pairs):
                entries.append((p, stacked_q(p, r0, hb)))
                starts.append(w0)
                rows0.append(r0)
        res = masked_pass(entries, starts, band, last_tile_mask(keep, band))
        mins = [None] * nq
        for (acc, stick), (p, _), r0 in zip(res, entries, rows0):
            write_out(acc, r0, hb, p)
            blk = r0 // tb
            mins[blk] = stick if mins[blk] is None else jnp.minimum(mins[blk], stick)
        return [jnp.min(m) for m in mins]

    def window_pass(blk_r0, w0, keep, aligned):
        if aligned:
            mask = last_tile_mask(keep, wk)
        else:
            def mask(a):
                return jnp.where(keep, a, 0.0)
        entries = [(p, stacked_q(p, blk_r0, tb)) for p in range(n_pairs)]
        res = masked_pass(entries, [w0] * n_pairs, wk, mask)
        for (acc, _), p in zip(res, range(n_pairs)):
            write_out(acc, blk_r0, tb, p)
        return res

    def block_step(j0, p, blk_r0, stick, acc, suffix):
        sp, ls = log_terms(_dot_nt(stacked_q(p, blk_r0, tb), pair(k_ref, j0, tb, p)))
        sums = _dot(sp.astype(BF16), suffix)
        wgt = jnp.exp(ls - sums[:, :tb] - stick)
        return stick + sums[:, tb:], acc + _dot(wgt.astype(BF16), pair(v_ref, j0, tb, p))

    def min_stick(sticks):
        m = sticks[0]
        for s in sticks[1:]:
            m = jnp.minimum(m, s)
        return jnp.min(m)

    def exact_block(blk):
        blk_r0 = blk * tb
        ib = i * nq + blk
        keep = lax.broadcasted_iota(jnp.int32, (2 * tb, LANES), 1) < query_row(tb, LANES) + (LANES - tb)
        res = window_pass(blk_r0, pl.multiple_of((ib - (window - 1)) * tb, tb), keep, True)
        for p in range(n_pairs):
            acc_ref[p] = res[p][0]
            stick_ref[p] = jnp.broadcast_to(res[p][1], (2 * tb, tb))
        suffix = jnp.concatenate([tri(tb), jnp.ones((tb, tb), F32)], axis=1).astype(BF16)

        def cond(st):
            return jnp.logical_and(st[0] >= 0, st[1] > 0)

        def body(st):
            j0 = pl.multiple_of(st[0] * tb, tb)
            new = []
            for p in range(n_pairs):
                stick, acc = block_step(j0, p, blk_r0, stick_ref[p], acc_ref[p], suffix)
                stick_ref[p] = stick
                acc_ref[p] = acc
                new.append(stick)
            return st[0] - 1, (min_stick(new) < DEAD_STICK).astype(jnp.int32)

        alive = (min_stick([r[1] for r in res]) < DEAD_STICK).astype(jnp.int32)
        lax.while_loop(cond, body, (ib - window, alive))
        for p in range(n_pairs):
            write_out(acc_ref[p], blk_r0, tb, p)

    @pl.when(i > 0)
    def _():
        for blk, lowest in enumerate(band_pass()):
            pl.when(lowest < DEAD_STICK)(functools.partial(exact_block, blk))

    @pl.when(i == 0)
    def _():
        for blk in range(nq):
            keep = lax.broadcasted_iota(jnp.int32, (2 * tb, wk), 1) < query_row(tb, wk) + blk * tb
            window_pass(blk * tb, 0, keep, False)


def _attn(q, k, v, *, tb=128, nq=4, window=4, band=256):
    b, s, w = q.shape
    tq = nq * tb
    assert s % tq == 0 and w % LANES == 0 and tq <= window * tb <= s
    assert band % LANES == 0 and tb // 2 <= LANES and band - tb // 2 <= tq and (window - 1) * tb <= tq
    n_pairs = w // LANES
    qspec = pl.BlockSpec((1, tq, w), lambda bi, i: (bi, i, 0))
    kvspec = pl.BlockSpec((1, s, w), lambda bi, i: (bi, 0, 0))
    return pl.pallas_call(
        functools.partial(_attn_body, tb=tb, nq=nq, window=window, band=band),
        grid=(b, s // tq),
        in_specs=[qspec, kvspec, kvspec],
        out_specs=qspec,
        out_shape=jax.ShapeDtypeStruct((b, s, w), BF16),
        scratch_shapes=[pltpu.VMEM((n_pairs, 2 * tb, tb), F32), pltpu.VMEM((n_pairs, 2 * tb, LANES), F32)],
        compiler_params=_params("parallel", "parallel"),
        name="stickbreak_attn",
    )(q, k, v)


def _merge_ffn_body(*refs, fc, n_blocks):
    n_cast = len(n_blocks)
    (x_ref, sb_ref, cv_ref, gl_ref, gates_ref, wb_ref, wo_ref, mpost_ref,
     pre_ref, post_ref, wg_ref, wu_ref, wd_ref) = refs[:13]
    cast_src = refs[13:13 + n_cast]
    o_ref = refs[13 + n_cast]
    cast_dst = refs[14 + n_cast:14 + 2 * n_cast]
    acc_ref = refs[14 + 2 * n_cast]
    _run_casts(pl.program_id(0), cast_src, cast_dst, n_blocks)
    d = x_ref.shape[1]
    w = sb_ref.shape[1]
    merged = gates_ref[:, 0:d].astype(F32) * _dot(sb_ref[...], wb_ref[0:w, :])
    merged = merged + gates_ref[:, d:2 * d].astype(F32) * _dot(cv_ref[...], wb_ref[w:2 * w, :])
    merged = merged + gates_ref[:, 2 * d:3 * d].astype(F32) * _dot(gl_ref[...], wb_ref[2 * w:3 * w, :])
    m = _dot(merged.astype(BF16), wo_ref[...])
    x = x_ref[...] + _rms(m, mpost_ref[...])
    o_ref[...] = _ffn_tile(x, pre_ref[...], post_ref[...], wg_ref, wu_ref, wd_ref, acc_ref, fc)


def _merge_ffn(x, sb, cv, gl, gates, wb, wo, mpost, pre, post, wg, wu, wd, cast_srcs, layer,
               *, tm=512, fc=512):
    t, d = x.shape
    w = sb.shape[1]
    c_in, c_out, c_shapes, n_blocks = _cast_jobs(cast_srcs, layer, lambda i: i, t // tm)

    def row(n):
        return pl.BlockSpec((tm, n), lambda i: (i, 0))

    vec = _resident((1, d))
    return pl.pallas_call(
        functools.partial(_merge_ffn_body, fc=fc, n_blocks=tuple(n_blocks)),
        grid=(t // tm,),
        in_specs=[row(d), row(w), row(w), row(w), row(3 * d), _resident(wb.shape),
                  _resident(wo.shape), vec, vec, vec,
                  _resident(wg.shape), _resident(wu.shape), _resident(wd.shape)] + c_in,
        out_specs=[row(d)] + c_out,
        out_shape=[jax.ShapeDtypeStruct((t, d), F32)] + c_shapes,
        scratch_shapes=[pltpu.VMEM((tm, d), F32)],
        compiler_params=_params("arbitrary"),
        name="merge_ffn",
    )(x, sb, cv, gl, gates, wb, wo, mpost.reshape(1, d), pre.reshape(1, d), post.reshape(1, d),
      wg, wu, wd, *cast_srcs)


def kernel(x, norm_pre, norm_post, ffn1_w_gate, ffn1_w_up, ffn1_w_down, ffn2_w_gate, ffn2_w_up,
           ffn2_w_down, w_in, conv_w, conv_b, conv_ln_g, conv_ln_b, gla_w_alpha, gla_b_alpha,
           gla_norm_g, w_branch, w_out):
    b, s, d = x.shape
    t = b * s
    n_layers = norm_pre.shape[0]
    walpha = jnp.pad(gla_w_alpha, ((0, 0), (0, LANES - GLA_GATE_RANK), (0, 0)))
    w_branch2 = w_branch.reshape(n_layers, N_BRANCHES * BRANCH_WIDTH, d)
    w_in_t = jnp.swapaxes(w_in, 1, 2)
    ffn1_srcs = [ffn1_w_gate, ffn1_w_up, ffn1_w_down]
    mix_srcs = [ffn2_w_gate, ffn2_w_up, ffn2_w_down, w_branch2, w_out]

    def flat(a):
        return a.reshape(t, a.shape[2])

    ffn1 = [a[0].astype(BF16) for a in ffn1_srcs]
    xf = x.reshape(t, d)
    for l in range(n_layers):
        xf, w_t = _ffn(xf, norm_pre[l, 0], norm_post[l, 0], *ffn1, w_in_t, l)
        q, k, v, gates, cv, gl, *mix_w = _mixer_in(
            xf.reshape(b, s, d), norm_pre[l, 1], w_t, walpha[l], gla_b_alpha[l],
            conv_w[l], conv_b[l], conv_ln_g[l], conv_ln_b[l], gla_norm_g[l], mix_srcs, l)
        sb = _attn(q, k, v)
        ffn2, (wb, wo) = mix_w[:3], mix_w[3:]
        nxt = ffn1_srcs if l + 1 < n_layers else []
        xf, *ffn1 = _merge_ffn(xf, flat(sb), flat(cv), flat(gl), flat(gates), wb, wo,
                               norm_post[l, 1], norm_pre[l, 2], norm_post[l, 2], *ffn2, nxt, l + 1)
    return xf.reshape(b, s, d)
```
